```python
import jax, jax.numpy as jnp
from jax import lax
import numpy as np

D_MODEL = 1024
BATCH = 8
SEQ = 4096
DEPTH = 1

PLE_DIM = 256
MIX_WIDTH = D_MODEL
HG_WIDTH = D_MODEL // 2
HG_HEADS = 4
HG_DK = HG_WIDTH // HG_HEADS
HG_DV = HG_WIDTH // HG_HEADS
HG_CHUNK = 64
SB_WIDTH = MIX_WIDTH - HG_WIDTH
SB_HEADS = 8
SB_DH = SB_WIDTH // SB_HEADS
SB_BLOCK = 128
D_FF = -(-8 * D_MODEL // (3 * 256)) * 256
IN_COLS = 4 * HG_WIDTH + 3 * SB_WIDTH
EPS = 1e-6

kernel_name = "hybrid_hgrn2_stickbreaking_block"


def rmsnorm(x, w):
    xf = x.astype(jnp.float32)
    xf = xf * lax.rsqrt(jnp.mean(xf * xf, axis=-1, keepdims=True) + EPS)
    return xf.astype(x.dtype) * w


def _to_chunks(a, heads, d):
    B, T, _ = a.shape
    return a.reshape(B, T // HG_CHUNK, HG_CHUNK, heads, d).transpose(1, 0, 3, 2, 4)


def hgrn2_mix(q, f_logit, i_in, lb):
    B, T, _ = q.shape
    q = jax.nn.silu(q.astype(jnp.float32))
    z = f_logit.astype(jnp.float32)
    log_f = jnp.logaddexp(jnp.log(lb), jnp.log1p(-lb) + jax.nn.log_sigmoid(z))
    k = -jnp.expm1(log_f)
    v = i_in.astype(jnp.float32)
    qc = _to_chunks(q, HG_HEADS, HG_DK)
    kc = _to_chunks(k, HG_HEADS, HG_DK)
    gc = _to_chunks(log_f, HG_HEADS, HG_DK)
    vc = _to_chunks(v, HG_HEADS, HG_DV)
    incl = jnp.tril(jnp.ones((HG_CHUNK, HG_CHUNK), dtype=bool))

    def step(S, inp):
        qb, kb, gb, vb = inp
        b = jnp.cumsum(gb, axis=2)
        o_inter = jnp.einsum('bhtk,bhkv->bhtv', qb * jnp.exp(b), S)
        diff = b[:, :, :, None, :] - b[:, :, None, :, :]
        decay = jnp.where(incl[:, :, None], jnp.exp(jnp.minimum(diff, 0.0)), 0.0)
        scores = jnp.einsum('bhtsk,bhsk->bhts', qb[:, :, :, None, :] * decay, kb)
        o_intra = jnp.einsum('bhts,bhsv->bhtv', scores, vb)
        b_last = b[:, :, -1:, :]
        S_new = jnp.exp(b_last[:, :, 0, :])[..., None] * S + jnp.einsum(
            'bhsk,bhsv->bhkv', kb * jnp.exp(b_last - b), vb)
        return S_new, o_inter + o_intra

    S0 = jnp.zeros((B, HG_HEADS, HG_DK, HG_DV), jnp.float32)
    _, oc = lax.scan(step, S0, (qc, kc, gc, vc))
    return oc.transpose(1, 0, 3, 2, 4).reshape(B, T, HG_HEADS * HG_DV)


def stick_breaking_mix(q, k, v):
    B, T, H, d = q.shape
    scale = d ** -0.5
    outs = []
    for blk in range(T // SB_BLOCK):
        t0 = blk * SB_BLOCK
        t1 = t0 + SB_BLOCK
        qb = q[:, t0:t1]
        kp = k[:, :t1]
        vp = v[:, :t1]
        z = jnp.einsum('bthd,bshd->bhts', qb, kp).astype(jnp.float32) * scale
        causal = jnp.arange(t1)[None, :] < jnp.arange(t0, t1)[:, None]
        log_1mb = jnp.where(causal, -jax.nn.softplus(z), 0.0)
        rem = lax.cumsum(log_1mb, axis=3, reverse=True) - log_1mb
        a = jnp.where(causal, jnp.exp(jax.nn.log_sigmoid(z) + rem), 0.0)
        outs.append(jnp.einsum('bhts,bshd->bthd', a.astype(v.dtype), vp))
    return jnp.concatenate(outs, axis=1)


def setup_inputs(seed: int = 0) -> dict:
    key = jax.random.key(seed)
    ks = jax.random.split(key, 16)
    f32 = jnp.float32
    nrm = lambda k, shape, s: jax.random.normal(k, shape, f32) * s
    gain = lambda k, shape: 1.0 + 0.05 * jax.random.normal(k, shape, f32)
    return {
        "x": jax.random.normal(ks[0], (BATCH, SEQ, D_MODEL), f32),
        "p": jax.random.normal(ks[1], (DEPTH, BATCH, SEQ, PLE_DIM), f32),
        "attn_pre_norm": gain(ks[2], (DEPTH, D_MODEL)),
        "w_in": nrm(ks[3], (DEPTH, D_MODEL, IN_COLS), D_MODEL ** -0.5),
        "hg_lower_gamma": nrm(ks[4], (DEPTH + 1, HG_WIDTH), 0.5),
        "hg_out_norm": gain(ks[5], (DEPTH, HG_WIDTH)),
        "sb_out_norm": gain(ks[6], (DEPTH, SB_WIDTH)),
        "w_out": nrm(ks[7], (DEPTH, MIX_WIDTH, D_MODEL), MIX_WIDTH ** -0.5),
        "attn_post_norm": gain(ks[8], (DEPTH, D_MODEL)),
        "ffn_pre_norm": gain(ks[9], (DEPTH, D_MODEL)),
        "w_gate_up": nrm(ks[10], (DEPTH, D_MODEL, 2 * D_FF), D_MODEL ** -0.5),
        "w_down": nrm(ks[11], (DEPTH, D_FF, D_MODEL), D_FF ** -0.5),
        "ffn_post_norm": gain(ks[12], (DEPTH, D_MODEL)),
        "ple_proj": nrm(ks[13], (DEPTH, PLE_DIM, D_MODEL), PLE_DIM ** -0.5),
        "ple_gate": nrm(ks[14], (DEPTH, D_MODEL, D_MODEL), D_MODEL ** -0.5),
    }


def reference(x, p, attn_pre_norm, w_in, hg_lower_gamma, hg_out_norm, sb_out_norm, w_out,
              attn_post_norm, ffn_pre_norm, w_gate_up, w_down, ffn_post_norm, ple_proj, ple_gate):
    B, T, _ = x.shape
    lb_all = jnp.cumsum(jax.nn.softmax(hg_lower_gamma.astype(jnp.float32), axis=0), axis=0)
    splits = [HG_WIDTH, 2 * HG_WIDTH, 3 * HG_WIDTH, 4 * HG_WIDTH,
              4 * HG_WIDTH + SB_WIDTH, 4 * HG_WIDTH + 2 * SB_WIDTH]
    h = x
    for i in range(DEPTH):
        u = rmsnorm(h, attn_pre_norm[i])
        proj = u @ w_in[i]
        hq, hf, hi, hg, sq, sk, sv = jnp.split(proj, splits, axis=-1)
        o_hg = hgrn2_mix(hq, hf, hi, lb_all[i])
        o_hg = rmsnorm(o_hg.astype(x.dtype), hg_out_norm[i]) * jax.nn.silu(hg)
        o_sb = stick_breaking_mix(sq.reshape(B, T, SB_HEADS, SB_DH),
                                  sk.reshape(B, T, SB_HEADS, SB_DH),
                                  sv.reshape(B, T, SB_HEADS, SB_DH)).reshape(B, T, SB_WIDTH)
        o_sb = rmsnorm(o_sb, sb_out_norm[i])
        mix = jnp.concatenate([o_hg, o_sb], axis=-1) @ w_out[i]
        h = h + rmsnorm(mix, attn_post_norm[i])
        u = rmsnorm(h, ffn_pre_norm[i])
        gate, up = jnp.split(u @ w_gate_up[i], [D_FF], axis=-1)
        y = (jax.nn.silu(gate) * up) @ w_down[i]
        h = h + rmsnorm(y, ffn_post_norm[i])
        h = h + (p[i] @ ple_proj[i]) * jax.nn.sigmoid(h @ ple_gate[i])
    return h
```

```python
import functools

import jax
import jax.numpy as jnp
from jax import lax
from jax.experimental import pallas as pl
from jax.experimental.pallas import tpu as pltpu

F32 = jnp.float32
BF16 = jnp.bfloat16
EPS = 1e-6

HG_HEADS = 4
SB_HEADS = 8
LANES = 128
HG_CHUNK = 128
HG_SUB = 16
SB_BLOCK = 128
FF_CHUNK = 256
VMEM_LIMIT = 56 * 1024 * 1024

_NT = (((1,), (1,)), ((), ()))
_TN = (((0,), (0,)), ((), ()))


def _rms(xf):
    return xf * lax.rsqrt(jnp.mean(xf * xf, axis=-1, keepdims=True) + EPS)


def _softplus(z):
    return jnp.maximum(z, 0.0) + jnp.log1p(jnp.exp(-jnp.abs(z)))


def _logaddexp(a, c):
    return jnp.maximum(a, c) + jnp.log1p(jnp.exp(-jnp.abs(a - c)))


def _const_spec(shape):
    nd = len(shape)
    return pl.BlockSpec(shape, lambda *_: (0,) * nd, pipeline_mode=pl.Buffered(1))


def _inproj_body(x_ref, nw_ref, w_ref, hq_ref, hf_ref, hi_ref, hg_ref, sq_ref, sk_ref, sv_ref,
                 *, width, sb_scale):
    u = (_rms(x_ref[...]) * nw_ref[...]).astype(BF16)
    outs = (hq_ref, hf_ref, hi_ref, hg_ref, sq_ref, sk_ref, sv_ref)
    for j, o_ref in enumerate(outs):
        r = jnp.dot(u, w_ref[:, j * width:(j + 1) * width], preferred_element_type=F32)
        if o_ref is sq_ref:
            r = r * sb_scale
        o_ref[...] = r.astype(o_ref.dtype)


def _inproj(x2, nw, w_in, *, width, sb_scale, tm):
    n, d = x2.shape
    out_dtypes = (BF16, F32, BF16, BF16, BF16, BF16, BF16)
    return pl.pallas_call(
        functools.partial(_inproj_body, width=width, sb_scale=sb_scale),
        grid=(n // tm,),
        in_specs=[
            pl.BlockSpec((tm, d), lambda i: (i, 0)),
            _const_spec((1, d)),
            _const_spec(w_in.shape),
        ],
        out_specs=[pl.BlockSpec((tm, width), lambda i: (i, 0)) for _ in out_dtypes],
        out_shape=[jax.ShapeDtypeStruct((n, width), dt) for dt in out_dtypes],
        compiler_params=pltpu.CompilerParams(
            dimension_semantics=("parallel",), vmem_limit_bytes=VMEM_LIMIT),
        name="inproj",
    )(x2, nw, w_in)


def _hgrn_body(hq_ref, hf_ref, hi_ref, hg_ref, gam_ref, onw_ref, tri_ref, esel_ref, o_ref,
               st_ref, b_scr, q_scr, k_scr, d_scr, o_scr, *, layer):
    c = HG_CHUNK
    nsub = c // HG_SUB

    @pl.when(pl.program_id(1) == 0)
    def _():
        st_ref[...] = jnp.zeros_like(st_ref)

    gam = gam_ref[...]
    e = jnp.exp(gam - jnp.max(gam, axis=0, keepdims=True))
    lb_all = jnp.sum(e[:layer + 1], axis=0, keepdims=True) / jnp.sum(e, axis=0, keepdims=True)

    col = lax.broadcasted_iota(jnp.int32, (HG_SUB, LANES), 1)
    rowl = lax.broadcasted_iota(jnp.int32, (HG_SUB, LANES), 0)
    tri = tri_ref[...]

    for h in range(HG_HEADS):
        cs = slice(h * LANES, (h + 1) * LANES)
        lb = lb_all[:, cs]
        qz = hq_ref[:, cs].astype(F32)
        q = qz * jax.nn.sigmoid(qz)
        z = hf_ref[:, cs]
        log_sig = jnp.minimum(z, 0.0) - jnp.log1p(jnp.exp(-jnp.abs(z)))
        logf = _logaddexp(jnp.log(lb), jnp.log1p(-lb) + log_sig)
        kk = 1.0 - jnp.exp(logf)
        v = hi_ref[:, cs]

        lhi = logf.astype(BF16)
        llo = (logf - lhi.astype(F32)).astype(BF16)
        b = (jnp.dot(tri, lhi, preferred_element_type=F32)
             + jnp.dot(tri, llo, preferred_element_type=F32))
        b_last = b[c - 1:c, :]

        st = st_ref[h]
        qh = (q * jnp.exp(b)).astype(BF16)
        o_h = lax.dot_general(qh, st.astype(BF16), _NT, preferred_element_type=F32)
        kh = (kk * jnp.exp(b_last - b)).astype(BF16)
        st_ref[h] = st * jnp.exp(b_last) + lax.dot_general(
            v, kh, _TN, preferred_element_type=F32)

        b_scr[...] = b
        q_scr[...] = q
        k_scr[...] = kk

        def sub_body(i, carry):
            r0 = pl.multiple_of(i * HG_SUB, HG_SUB)
            bi = b_scr[pl.ds(r0, HG_SUB), :]
            qi = q_scr[pl.ds(r0, HG_SUB), :]
            for s in range(HG_SUB):
                brow = b_scr[pl.ds(r0 + s, 1), :]
                krow = k_scr[pl.ds(r0 + s, 1), :]
                d = qi * krow * jnp.exp(jnp.minimum(bi - brow, 0.0))
                d_scr[pl.ds(r0, HG_SUB), s * LANES:(s + 1) * LANES] = d.astype(BF16)
            return carry

        lax.fori_loop(0, nsub, sub_body, 0)
        sdiag = jnp.dot(d_scr[...], esel_ref[...], preferred_element_type=F32)

        rows = []
        for i in range(nsub):
            r0 = i * HG_SUB
            keep = (col >= r0) & (col <= r0 + rowl)
            sc = jnp.where(keep, sdiag[r0:r0 + HG_SUB], 0.0)
            if i > 0:
                r = b[r0 - 1:r0, :]
                qp = (q[r0:r0 + HG_SUB] * jnp.exp(b[r0:r0 + HG_SUB] - r)).astype(BF16)
                kp = (kk[:r0] * jnp.exp(r - b[:r0])).astype(BF16)
                kp = jnp.concatenate([kp, jnp.zeros((c - r0, LANES), BF16)], axis=0)
                sc = sc + lax.dot_general(qp, kp, _NT, preferred_element_type=F32)
            rows.append(sc)
        scores = jnp.concatenate(rows, axis=0).astype(BF16)
        o_h = o_h + jnp.dot(scores, v, preferred_element_type=F32)
        o_scr[:, cs] = o_h

    o = _rms(o_scr[...]) * onw_ref[...]
    g = hg_ref[...].astype(F32)
    o_ref[...] = (o * (g * jax.nn.sigmoid(g))).astype(o_ref.dtype)


def _hgrn(hq, hf, hi, hg, gamma, onw, *, layer):
    bsz, t, w = hq.shape
    c = HG_CHUNK
    tri = jnp.tril(jnp.ones((c, c), F32)).astype(BF16)
    sel = (jnp.arange(HG_SUB * LANES)[:, None] // LANES) == (jnp.arange(LANES)[None, :] % HG_SUB)
    esel = sel.astype(BF16)
    blk = pl.BlockSpec((None, c, w), lambda b, i: (b, i, 0))
    return pl.pallas_call(
        functools.partial(_hgrn_body, layer=layer),
        grid=(bsz, t // c),
        in_specs=[blk, blk, blk, blk,
                  _const_spec(gamma.shape), _const_spec(onw.shape),
                  _const_spec(tri.shape), _const_spec(esel.shape)],
        out_specs=blk,
        out_shape=jax.ShapeDtypeStruct((bsz, t, w), BF16),
        scratch_shapes=[
            pltpu.VMEM((HG_HEADS, LANES, LANES), F32),
            pltpu.VMEM((c, LANES), F32),
            pltpu.VMEM((c, LANES), F32),
            pltpu.VMEM((c, LANES), F32),
            pltpu.VMEM((c, HG_SUB * LANES), BF16),
            pltpu.VMEM((c, w), F32),
        ],
        compiler_params=pltpu.CompilerParams(
            dimension_semantics=("parallel", "arbitrary"), vmem_limit_bytes=VMEM_LIMIT),
        name="hgrn2",
    )(hq, hf, hi, hg, gamma, onw, tri, esel)


def _sb_body(q_ref, k_ref, v_ref, m_ref, o_ref, *, dh):
    blk = SB_BLOCK
    qi = pl.program_id(2)
    q = q_ref[...]
    lane = lax.broadcasted_iota(jnp.int32, (blk, LANES), 1)
    row = lax.broadcasted_iota(jnp.int32, (blk, blk), 0)
    colk = lax.broadcasted_iota(jnp.int32, (blk, blk), 1)
    strict = colk < row
    mcat = m_ref[...]

    outs = []
    for h in range(LANES // dh):
        in_head = (lane >= h * dh) & (lane < (h + 1) * dh)
        qh = jnp.where(in_head, q, jnp.zeros_like(q))

        def block(kb, o, c, masked):
            start = pl.multiple_of(kb * blk, blk)
            k = k_ref[pl.ds(start, blk), :]
            v = v_ref[pl.ds(start, blk), :]
            z = lax.dot_general(qh, k, _NT, preferred_element_type=F32)
            sp = _softplus(z)
            spm = jnp.where(strict, sp, 0.0) if masked else sp
            rt = jnp.dot(spm.astype(BF16), mcat, preferred_element_type=F32)
            a = jnp.exp(z - sp - rt[:, :blk] - c)
            if masked:
                a = jnp.where(strict, a, 0.0)
            o = o + jnp.dot(a.astype(BF16), v, preferred_element_type=F32)
            return o, c + rt[:, blk:]

        zero = jnp.zeros((blk, LANES), F32)
        o, c = block(qi, zero, zero, True)
        o, c = lax.fori_loop(1, qi + 1, lambda j, oc: block(qi - j, oc[0], oc[1], False), (o, c))
        outs.append((in_head, o))

    res = outs[-1][1]
    for in_head, o in outs[:-1]:
        res = jnp.where(in_head, o, res)
    o_ref[...] = res.astype(o_ref.dtype)


def _stickbreak(sq, sk, sv, *, dh):
    bsz, t, w = sq.shape
    blk = SB_BLOCK
    later = jnp.arange(blk)[:, None] > jnp.arange(blk)[None, :]
    mcat = jnp.concatenate([later, jnp.ones((blk, LANES), bool)], axis=1).astype(BF16)
    qspec = pl.BlockSpec((None, blk, LANES), lambda b, hp, i: (b, i, hp))
    kvspec = pl.BlockSpec((None, t, LANES), lambda b, hp, i: (b, 0, hp))
    return pl.pallas_call(
        functools.partial(_sb_body, dh=dh),
        grid=(bsz, w // LANES, t // blk),
        in_specs=[qspec, kvspec, kvspec, _const_spec(mcat.shape)],
        out_specs=qspec,
        out_shape=jax.ShapeDtypeStruct((bsz, t, w), BF16),
        compiler_params=pltpu.CompilerParams(
            dimension_semantics=("parallel", "parallel", "arbitrary"),
            vmem_limit_bytes=VMEM_LIMIT),
        name="stickbreak",
    )(sq, sk, sv, mcat)


def _tail_body(x_ref, ohg_ref, osb_ref, p_ref, sbn_ref, apn_ref, fpre_ref, fpost_ref,
               wout_ref, wg_ref, wu_ref, wd_ref, pp_ref, pg_ref, o_ref, u_scr, act_scr):
    osb = (_rms(osb_ref[...].astype(F32)) * sbn_ref[...]).astype(BF16)
    mix_in = jnp.concatenate([ohg_ref[...], osb], axis=-1)
    mix = jnp.dot(mix_in, wout_ref[...], preferred_element_type=F32)
    h1 = x_ref[...] + _rms(mix) * apn_ref[...]
    u_scr[...] = (_rms(h1) * fpre_ref[...]).astype(BF16)

    def ff_chunk(j, carry):
        u = u_scr[...]
        g = jnp.dot(u, wg_ref[j], preferred_element_type=F32)
        up = jnp.dot(u, wu_ref[j], preferred_element_type=F32)
        act_scr[j] = (g * jax.nn.sigmoid(g) * up).astype(BF16)
        return carry

    n_chunks = wg_ref.shape[0]
    lax.fori_loop(0, n_chunks, ff_chunk, 0)

    def down_chunk(j, y):
        return y + jnp.dot(act_scr[j], wd_ref[j], preferred_element_type=F32)

    y = lax.fori_loop(0, n_chunks, down_chunk, jnp.zeros(h1.shape, F32))
    h2 = h1 + _rms(y) * fpost_ref[...]
    emb = jnp.dot(p_ref[...].astype(BF16), pp_ref[...], preferred_element_type=F32)
    gate = jnp.dot(h2.astype(BF16), pg_ref[...], preferred_element_type=F32)
    o_ref[...] = h2 + emb * jax.nn.sigmoid(gate)


def _tail(x2, ohg, osb, p2, sbn, apn, fpre, fpost, wout, wg, wu, wd, pp, pg, *, tm):
    n, d = x2.shape
    row = lambda a: pl.BlockSpec((tm, a.shape[1]), lambda i: (i, 0))
    consts = (sbn, apn, fpre, fpost, wout, wg, wu, wd, pp, pg)
    return pl.pallas_call(
        _tail_body,
        grid=(n // tm,),
        in_specs=[row(x2), row(ohg), row(osb), row(p2)] + [_const_spec(a.shape) for a in consts],
        out_specs=row(x2),
        out_shape=jax.ShapeDtypeStruct((n, d), F32),
        scratch_shapes=[
            pltpu.VMEM((tm, d), BF16),
            pltpu.VMEM((wg.shape[0], tm, FF_CHUNK), BF16),
        ],
        compiler_params=pltpu.CompilerParams(
            dimension_semantics=("parallel",), vmem_limit_bytes=VMEM_LIMIT),
        name="tail",
    )(x2, ohg, osb, p2, *consts)


def kernel(x, p, attn_pre_norm, w_in, hg_lower_gamma, hg_out_norm, sb_out_norm, w_out, attn_post_norm, ffn_pre_norm, w_gate_up, w_down, ffn_post_norm, ple_proj, ple_gate):
    bsz, t, d = x.shape
    depth = w_in.shape[0]
    n = bsz * t
    width = hg_out_norm.shape[1]
    assert sb_out_norm.shape[1] == width and w_in.shape[2] == 7 * width
    assert width == HG_HEADS * LANES and t % HG_CHUNK == 0 and t % SB_BLOCK == 0
    dh = width // SB_HEADS
    d_ff = w_down.shape[1]
    assert d_ff % FF_CHUNK == 0
    n_ff = d_ff // FF_CHUNK
    tm = 512
    assert n % tm == 0

    h = x.reshape(n, d)
    for i in range(depth):
        hq, hf, hi, hg, sq, sk, sv = _inproj(
            h, attn_pre_norm[i][None], w_in[i].astype(BF16),
            width=width, sb_scale=dh ** -0.5, tm=tm)
        to3 = lambda a: a.reshape(bsz, t, width)
        o_hg = _hgrn(to3(hq), to3(hf), to3(hi), to3(hg), hg_lower_gamma,
                     hg_out_norm[i][None], layer=i)
        o_sb = _stickbreak(to3(sq), to3(sk), to3(sv), dh=dh)
        wgu = w_gate_up[i].astype(BF16)
        split = lambda w: w.reshape(d, n_ff, FF_CHUNK).transpose(1, 0, 2)
        h = _tail(
            h, o_hg.reshape(n, width), o_sb.reshape(n, width), p[i].reshape(n, -1),
            sb_out_norm[i][None], attn_post_norm[i][None], ffn_pre_norm[i][None],
            ffn_post_norm[i][None], w_out[i].astype(BF16),
            split(wgu[:, :d_ff]), split(wgu[:, d_ff:]),
            w_down[i].astype(BF16).reshape(n_ff, FF_CHUNK, d),
            ple_proj[i].astype(BF16), ple_gate[i].astype(BF16), tm=tm)
    return h.reshape(bsz, t, d)
```

```python
import functools

import jax
import jax.numpy as jnp
from jax import lax
from jax.experimental import pallas as pl
from jax.experimental.pallas import tpu as pltpu

F32 = jnp.float32
BF16 = jnp.bfloat16
EPS = 1e-6

HG_HEADS = 4
SB_HEADS = 8
LANES = 128
HG_CHUNK = 128
HG_SUB = 16
SB_BLOCK = 128
SB_LOOK = 2
SB_DONE = 115.0
SB_OFF = 1e30
FF_CHUNK = 256
VMEM_LIMIT = 56 * 1024 * 1024

_NT = (((1,), (1,)), ((), ()))
_TN = (((0,), (0,)), ((), ()))


def _rms(xf):
    return xf * lax.rsqrt(jnp.mean(xf * xf, axis=-1, keepdims=True) + EPS)


def _softplus(z):
    return jnp.maximum(z, 0.0) + jnp.log1p(jnp.exp(-jnp.abs(z)))


def _logaddexp(a, c):
    return jnp.maximum(a, c) + jnp.log1p(jnp.exp(-jnp.abs(a - c)))


def _const_spec(shape):
    nd = len(shape)
    return pl.BlockSpec(shape, lambda *_: (0,) * nd, pipeline_mode=pl.Buffered(1))


def _inproj_body(x_ref, nw_ref, w_ref, hq_ref, hf_ref, hi_ref, hg_ref, sq_ref, sk_ref, sv_ref,
                 *, width, sb_scale):
    u = (_rms(x_ref[...]) * nw_ref[...]).astype(BF16)
    outs = (hq_ref, hf_ref, hi_ref, hg_ref, sq_ref, sk_ref, sv_ref)
    for j, o_ref in enumerate(outs):
        r = jnp.dot(u, w_ref[:, j * width:(j + 1) * width], preferred_element_type=F32)
        if o_ref is sq_ref:
            r = r * sb_scale
        o_ref[...] = r.astype(o_ref.dtype)


def _inproj(x2, nw, w_in, *, width, sb_scale, tm):
    n, d = x2.shape
    out_dtypes = (BF16, F32, BF16, BF16, BF16, BF16, BF16)
    return pl.pallas_call(
        functools.partial(_inproj_body, width=width, sb_scale=sb_scale),
        grid=(n // tm,),
        in_specs=[
            pl.BlockSpec((tm, d), lambda i: (i, 0)),
            _const_spec((1, d)),
            _const_spec(w_in.shape),
        ],
        out_specs=[pl.BlockSpec((tm, width), lambda i: (i, 0)) for _ in out_dtypes],
        out_shape=[jax.ShapeDtypeStruct((n, width), dt) for dt in out_dtypes],
        compiler_params=pltpu.CompilerParams(
            dimension_semantics=("parallel",), vmem_limit_bytes=VMEM_LIMIT),
        name="inproj",
    )(x2, nw, w_in)


def _hgrn_body(hq_ref, hf_ref, hi_ref, hg_ref, gam_ref, onw_ref, tri_ref, esel_ref, o_ref,
               st_ref, b_scr, q_scr, k_scr, d_scr, o_scr, *, layer):
    c = HG_CHUNK
    nsub = c // HG_SUB

    @pl.when(pl.program_id(1) == 0)
    def _():
        st_ref[...] = jnp.zeros_like(st_ref)

    gam = gam_ref[...]
    e = jnp.exp(gam - jnp.max(gam, axis=0, keepdims=True))
    lb_all = jnp.sum(e[:layer + 1], axis=0, keepdims=True) / jnp.sum(e, axis=0, keepdims=True)

    col = lax.broadcasted_iota(jnp.int32, (HG_SUB, LANES), 1)
    rowl = lax.broadcasted_iota(jnp.int32, (HG_SUB, LANES), 0)
    tri = tri_ref[...]

    for h in range(HG_HEADS):
        cs = slice(h * LANES, (h + 1) * LANES)
        lb = lb_all[:, cs]
        qz = hq_ref[:, cs].astype(F32)
        q = qz * jax.nn.sigmoid(qz)
        z = hf_ref[:, cs]
        log_sig = jnp.minimum(z, 0.0) - jnp.log1p(jnp.exp(-jnp.abs(z)))
        logf = _logaddexp(jnp.log(lb), jnp.log1p(-lb) + log_sig)
        kk = 1.0 - jnp.exp(logf)
        v = hi_ref[:, cs]

        lhi = logf.astype(BF16)
        llo = (logf - lhi.astype(F32)).astype(BF16)
        b = (jnp.dot(tri, lhi, preferred_element_type=F32)
             + jnp.dot(tri, llo, preferred_element_type=F32))
        b_last = b[c - 1:c, :]

        st = st_ref[h]
        qh = (q * jnp.exp(b)).astype(BF16)
        o_h = lax.dot_general(qh, st.astype(BF16), _NT, preferred_element_type=F32)
        kh = (kk * jnp.exp(b_last - b)).astype(BF16)
        st_ref[h] = st * jnp.exp(b_last) + lax.dot_general(
            v, kh, _TN, preferred_element_type=F32)

        b_scr[...] = b
        q_scr[...] = q
        k_scr[...] = kk

        def sub_body(i, carry):
            r0 = pl.multiple_of(i * HG_SUB, HG_SUB)
            bi = b_scr[pl.ds(r0, HG_SUB), :]
            qi = q_scr[pl.ds(r0, HG_SUB), :]
            for s in range(HG_SUB):
                brow = b_scr[pl.ds(r0 + s, 1), :]
                krow = k_scr[pl.ds(r0 + s, 1), :]
                d = qi * krow * jnp.exp(jnp.minimum(bi - brow, 0.0))
                d_scr[pl.ds(r0, HG_SUB), s * LANES:(s + 1) * LANES] = d.astype(BF16)
            return carry

        lax.fori_loop(0, nsub, sub_body, 0)
        sdiag = jnp.dot(d_scr[...], esel_ref[...], preferred_element_type=F32)

        rows = []
        for i in range(nsub):
            r0 = i * HG_SUB
            keep = (col >= r0) & (col <= r0 + rowl)
            sc = jnp.where(keep, sdiag[r0:r0 + HG_SUB], 0.0)
            if i > 0:
                r = b[r0 - 1:r0, :]
                qp = (q[r0:r0 + HG_SUB] * jnp.exp(b[r0:r0 + HG_SUB] - r)).astype(BF16)
                kp = (kk[:r0] * jnp.exp(r - b[:r0])).astype(BF16)
                kp = jnp.concatenate([kp, jnp.zeros((c - r0, LANES), BF16)], axis=0)
                sc = sc + lax.dot_general(qp, kp, _NT, preferred_element_type=F32)
            rows.append(sc)
        scores = jnp.concatenate(rows, axis=0).astype(BF16)
        o_h = o_h + jnp.dot(scores, v, preferred_element_type=F32)
        o_scr[:, cs] = o_h

    o = _rms(o_scr[...]) * onw_ref[...]
    g = hg_ref[...].astype(F32)
    o_ref[...] = (o * (g * jax.nn.sigmoid(g))).astype(o_ref.dtype)


def _hgrn(hq, hf, hi, hg, gamma, onw, *, layer):
    bsz, t, w = hq.shape
    c = HG_CHUNK
    tri = jnp.tril(jnp.ones((c, c), F32)).astype(BF16)
    sel = (jnp.arange(HG_SUB * LANES)[:, None] // LANES) == (jnp.arange(LANES)[None, :] % HG_SUB)
    esel = sel.astype(BF16)
    blk = pl.BlockSpec((None, c, w), lambda b, i: (b, i, 0))
    return pl.pallas_call(
        functools.partial(_hgrn_body, layer=layer),
        grid=(bsz, t // c),
        in_specs=[blk, blk, blk, blk,
                  _const_spec(gamma.shape), _const_spec(onw.shape),
                  _const_spec(tri.shape), _const_spec(esel.shape)],
        out_specs=blk,
        out_shape=jax.ShapeDtypeStruct((bsz, t, w), BF16),
        scratch_shapes=[
            pltpu.VMEM((HG_HEADS, LANES, LANES), F32),
            pltpu.VMEM((c, LANES), F32),
            pltpu.VMEM((c, LANES), F32),
            pltpu.VMEM((c, LANES), F32),
            pltpu.VMEM((c, HG_SUB * LANES), BF16),
            pltpu.VMEM((c, w), F32),
        ],
        compiler_params=pltpu.CompilerParams(
            dimension_semantics=("parallel", "arbitrary"), vmem_limit_bytes=VMEM_LIMIT),
        name="hgrn2",
    )(hq, hf, hi, hg, gamma, onw, tri, esel)


def _sb_body(q_ref, k_ref, v_ref, m_ref, nw_ref, o_ref, o_scr, c_scr, *, dh):
    blk = SB_BLOCK
    nh = LANES // dh
    n0 = nh * blk
    npairs = q_ref.shape[1] // LANES
    nwin = SB_LOOK + 1
    qi = pl.program_id(1)
    lane = lax.broadcasted_iota(jnp.int32, (blk, LANES), 1)
    row = lax.broadcasted_iota(jnp.int32, (n0, blk), 0)
    colk = lax.broadcasted_iota(jnp.int32, (n0, blk), 1)
    strict = colk < (row & (blk - 1))
    mcat = m_ref[...]

    def in_head(h):
        return (lane >= h * dh) & (lane < (h + 1) * dh)

    def head_rows(q):
        return jnp.concatenate(
            [jnp.where(in_head(h), q, jnp.zeros_like(q)) for h in range(nh)], axis=0)

    cmin = None
    for hp in range(npairs):
        cs = slice(hp * LANES, (hp + 1) * LANES)
        qs = head_rows(q_ref[:, cs])
        ks, vs = [], []
        for j in range(nwin):
            start = pl.multiple_of(jnp.maximum(qi - j, 0) * blk, blk)
            ks.append(k_ref[pl.ds(start, blk), cs])
            vs.append(v_ref[pl.ds(start, blk), cs])
        z = lax.dot_general(qs, jnp.concatenate(ks, axis=0), _NT, preferred_element_type=F32)
        z = jnp.concatenate([z[:, j * blk:(j + 1) * blk] for j in range(nwin)], axis=0)
        sp = _softplus(z)
        spm = jnp.concatenate([jnp.where(strict, sp[:n0], 0.0), sp[n0:]], axis=0)
        rt = jnp.dot(spm.astype(BF16), mcat, preferred_element_type=F32)
        e = z - sp - rt[:, :blk]
        c = jnp.zeros((n0, blk), F32)
        a_parts = []
        for j in range(nwin):
            sl = slice(j * n0, (j + 1) * n0)
            if j == 0:
                a = jnp.where(strict, jnp.exp(e[sl]), 0.0)
            else:
                a = jnp.exp(e[sl] - (c + jnp.where(qi >= j, 0.0, SB_OFF)))
            a_parts.append(a.astype(BF16))
            c = c + rt[sl, blk:]
        o_scr[hp] = jnp.dot(jnp.concatenate(a_parts, axis=1), jnp.concatenate(vs, axis=0),
                            preferred_element_type=F32)
        c_scr[hp] = c
        cmin = c if cmin is None else jnp.minimum(cmin, c)

    def more(carry):
        kb, cm = carry
        return (kb >= 0) & (cm < SB_DONE)

    def one_block(carry):
        kb, _ = carry
        start = pl.multiple_of(kb * blk, blk)
        cmin = None
        for hp in range(npairs):
            cs = slice(hp * LANES, (hp + 1) * LANES)
            qs = head_rows(q_ref[:, cs])
            z = lax.dot_general(qs, k_ref[pl.ds(start, blk), cs], _NT, preferred_element_type=F32)
            sp = _softplus(z)
            rt = jnp.dot(sp.astype(BF16), mcat, preferred_element_type=F32)
            c = c_scr[hp]
            a = jnp.exp(z - sp - rt[:, :blk] - c)
            o_scr[hp] += jnp.dot(a.astype(BF16), v_ref[pl.ds(start, blk), cs],
                                 preferred_element_type=F32)
            c = c + rt[:, blk:]
            c_scr[hp] = c
            cmin = c if cmin is None else jnp.minimum(cmin, c)
        return kb - 1, jnp.min(cmin)

    lax.while_loop(more, one_block, (qi - nwin, jnp.min(cmin)))

    parts = []
    for hp in range(npairs):
        o2 = o_scr[hp]
        res = o2[(nh - 1) * blk:]
        for h in range(nh - 1):
            res = jnp.where(in_head(h), o2[h * blk:(h + 1) * blk], res)
        parts.append(res)
    o = jnp.concatenate(parts, axis=1)
    o_ref[...] = (_rms(o) * nw_ref[...]).astype(o_ref.dtype)


def _stickbreak(sq, sk, sv, nw, *, dh):
    bsz, t, w = sq.shape
    blk = SB_BLOCK
    later = jnp.arange(blk)[:, None] > jnp.arange(blk)[None, :]
    mcat = jnp.concatenate([later, jnp.ones((blk, blk), bool)], axis=1).astype(BF16)
    qspec = pl.BlockSpec((None, blk, w), lambda b, i: (b, i, 0))
    kvspec = pl.BlockSpec((None, t, w), lambda b, i: (b, 0, 0))
    nh = LANES // dh
    return pl.pallas_call(
        functools.partial(_sb_body, dh=dh),
        grid=(bsz, t // blk),
        in_specs=[qspec, kvspec, kvspec, _const_spec(mcat.shape), _const_spec(nw.shape)],
        out_specs=qspec,
        out_shape=jax.ShapeDtypeStruct((bsz, t, w), BF16),
        scratch_shapes=[
            pltpu.VMEM((w // LANES, nh * blk, LANES), F32),
            pltpu.VMEM((w // LANES, nh * blk, blk), F32),
        ],
        compiler_params=pltpu.CompilerParams(
            dimension_semantics=("parallel", "arbitrary"), vmem_limit_bytes=VMEM_LIMIT),
        name="stickbreak",
    )(sq, sk, sv, mcat, nw)


def _tail_body(x_ref, ohg_ref, osb_ref, p_ref, apn_ref, fpre_ref, fpost_ref,
               wout_ref, wg_ref, wu_ref, wd_ref, pp_ref, pg_ref, o_ref, u_scr, act_scr):
    mix_in = jnp.concatenate([ohg_ref[...], osb_ref[...]], axis=-1)
    mix = jnp.dot(mix_in, wout_ref[...], preferred_element_type=F32)
    h1 = x_ref[...] + _rms(mix) * apn_ref[...]
    u_scr[...] = (_rms(h1) * fpre_ref[...]).astype(BF16)

    def ff_chunk(j, carry):
        u = u_scr[...]
        g = jnp.dot(u, wg_ref[j], preferred_element_type=F32)
        up = jnp.dot(u, wu_ref[j], preferred_element_type=F32)
        act_scr[j] = (g * jax.nn.sigmoid(g) * up).astype(BF16)
        return carry

    n_chunks = wg_ref.shape[0]
    lax.fori_loop(0, n_chunks, ff_chunk, 0)

    def down_chunk(j, y):
        return y + jnp.dot(act_scr[j], wd_ref[j], preferred_element_type=F32)

    y = lax.fori_loop(0, n_chunks, down_chunk, jnp.zeros(h1.shape, F32))
    h2 = h1 + _rms(y) * fpost_ref[...]
    emb = jnp.dot(p_ref[...].astype(BF16), pp_ref[...], preferred_element_type=F32)
    gate = jnp.dot(h2.astype(BF16), pg_ref[...], preferred_element_type=F32)
    o_ref[...] = h2 + emb * jax.nn.sigmoid(gate)


def _tail(x2, ohg, osb, p2, apn, fpre, fpost, wout, wg, wu, wd, pp, pg, *, tm):
    n, d = x2.shape
    row = lambda a: pl.BlockSpec((tm, a.shape[1]), lambda i: (i, 0))
    consts = (apn, fpre, fpost, wout, wg, wu, wd, pp, pg)
    return pl.pallas_call(
        _tail_body,
        grid=(n // tm,),
        in_specs=[row(x2), row(ohg), row(osb), row(p2)] + [_const_spec(a.shape) for a in consts],
        out_specs=row(x2),
        out_shape=jax.ShapeDtypeStruct((n, d), F32),
        scratch_shapes=[
            pltpu.VMEM((tm, d), BF16),
            pltpu.VMEM((wg.shape[0], tm, FF_CHUNK), BF16),
        ],
        compiler_params=pltpu.CompilerParams(
            dimension_semantics=("parallel",), vmem_limit_bytes=VMEM_LIMIT),
        name="tail",
    )(x2, ohg, osb, p2, *consts)


def kernel(x, p, attn_pre_norm, w_in, hg_lower_gamma, hg_out_norm, sb_out_norm, w_out, attn_post_norm, ffn_pre_norm, w_gate_up, w_down, ffn_post_norm, ple_proj, ple_gate):
    bsz, t, d = x.shape
    depth = w_in.shape[0]
    n = bsz * t
    width = hg_out_norm.shape[1]
    assert sb_out_norm.shape[1] == width and w_in.shape[2] == 7 * width
    assert width == HG_HEADS * LANES and t % HG_CHUNK == 0 and t % SB_BLOCK == 0
    dh = width // SB_HEADS
    d_ff = w_down.shape[1]
    assert d_ff % FF_CHUNK == 0
    n_ff = d_ff // FF_CHUNK
    tm = 512
    assert n % tm == 0

    h = x.reshape(n, d)
    for i in range(depth):
        hq, hf, hi, hg, sq, sk, sv = _inproj(
            h, attn_pre_norm[i][None], w_in[i].astype(BF16),
            width=width, sb_scale=dh ** -0.5, tm=tm)
        to3 = lambda a: a.reshape(bsz, t, width)
        o_hg = _hgrn(to3(hq), to3(hf), to3(hi), to3(hg), hg_lower_gamma,
                     hg_out_norm[i][None], layer=i)
        o_sb = _stickbreak(to3(sq), to3(sk), to3(sv), sb_out_norm[i][None], dh=dh)
        wgu = w_gate_up[i].astype(BF16)
        split = lambda w: w.reshape(d, n_ff, FF_CHUNK).transpose(1, 0, 2)
        h = _tail(
            h, o_hg.reshape(n, width), o_sb.reshape(n, width), p[i].reshape(n, -1),
            attn_post_norm[i][None], ffn_pre_norm[i][None],
            ffn_post_norm[i][None], w_out[i].astype(BF16),
            split(wgu[:, :d_ff]), split(wgu[:, d_ff:]),
            w_down[i].astype(BF16).reshape(n_ff, FF_CHUNK, d),
            ple_proj[i].astype(BF16), ple_gate[i].astype(BF16), tm=tm)
    return h.reshape(bsz, t, d)
```

```python
import functools

import jax
import jax.numpy as jnp
from jax import lax
from jax.experimental import pallas as pl
from jax.experimental.pallas import tpu as pltpu

F32 = jnp.float32
BF16 = jnp.bfloat16
EPS = 1e-6
LOG2E = 1.4426950408889634

HG_HEADS = 4
SB_HEADS = 8
LANES = 128
HG_CHUNK = 128
HG_SUB = 8
HG_SEQS = 2
SB_BLOCK = 128
SB_LOOK = 2
SB_DONE = 115.0
SB_OFF = 1e30
FF_SPLIT = 2
VMEM_LIMIT = 56 * 1024 * 1024

_NT = (((1,), (1,)), ((), ()))
_TN = (((0,), (0,)), ((), ()))


def _rms(xf):
    return xf * lax.rsqrt(jnp.mean(xf * xf, axis=-1, keepdims=True) + EPS)


def _softplus(z):
    return jnp.maximum(z, 0.0) + jnp.log(1.0 + jnp.exp(-jnp.abs(z)))


def _logaddexp(a, c):
    return jnp.maximum(a, c) + jnp.log(1.0 + jnp.exp(-jnp.abs(a - c)))


def _const_spec(shape):
    nd = len(shape)
    return pl.BlockSpec(shape, lambda *_: (0,) * nd, pipeline_mode=pl.Buffered(1))


def _inproj_body(x_ref, nw_ref, w_ref, hq_ref, hf_ref, hi_ref, hg_ref, sq_ref, sk_ref, sv_ref,
                 *, width, sb_scale):
    u = (_rms(x_ref[...]) * nw_ref[...]).astype(BF16)
    outs = (hq_ref, hf_ref, hi_ref, hg_ref, sq_ref, sk_ref, sv_ref)
    for j, o_ref in enumerate(outs):
        r = jnp.dot(u, w_ref[:, j * width:(j + 1) * width], preferred_element_type=F32)
        if o_ref is sq_ref:
            r = r * sb_scale
        o_ref[...] = r.astype(o_ref.dtype)


def _inproj(x2, nw, w_in, *, width, sb_scale, tm):
    n, d = x2.shape
    out_dtypes = (BF16, F32, BF16, BF16, BF16, BF16, BF16)
    return pl.pallas_call(
        functools.partial(_inproj_body, width=width, sb_scale=sb_scale),
        grid=(n // tm,),
        in_specs=[
            pl.BlockSpec((tm, d), lambda i: (i, 0)),
            _const_spec((1, d)),
            _const_spec(w_in.shape),
        ],
        out_specs=[pl.BlockSpec((tm, width), lambda i: (i, 0)) for _ in out_dtypes],
        out_shape=[jax.ShapeDtypeStruct((n, width), dt) for dt in out_dtypes],
        compiler_params=pltpu.CompilerParams(
            dimension_semantics=("parallel",), vmem_limit_bytes=VMEM_LIMIT),
        name="inproj",
    )(x2, nw, w_in)


def _hgrn_body(hq_ref, hf_ref, hi_ref, hg_ref, gam_ref, onw_ref, tri_ref, esel_ref, o_ref,
               st_ref, b_scr, lk_scr, k_scr, d_scr, *, layer):
    @pl.when(pl.program_id(1) == 0)
    def _():
        st_ref[...] = jnp.zeros_like(st_ref)

    for i in range(hq_ref.shape[0]):
        _hgrn_chunk(hq_ref.at[i], hf_ref.at[i], hi_ref.at[i], hg_ref.at[i], gam_ref, onw_ref,
                    tri_ref, esel_ref, o_ref.at[i], st_ref.at[i], b_scr.at[i], lk_scr.at[i],
                    k_scr.at[i], d_scr.at[i], layer=layer)


def _hgrn_chunk(hq_ref, hf_ref, hi_ref, hg_ref, gam_ref, onw_ref, tri_ref, esel_ref, o_ref,
                st_ref, b_scr, lk_scr, k_scr, d_scr, *, layer):
    c = HG_CHUNK
    sub = HG_SUB

    gam = gam_ref[...]
    e = jnp.exp(gam - jnp.max(gam, axis=0, keepdims=True))
    lb = jnp.sum(e[:layer + 1], axis=0, keepdims=True) / jnp.sum(e, axis=0, keepdims=True)

    qz = hq_ref[...].astype(F32)
    q = qz * jax.nn.sigmoid(qz)
    z = hf_ref[...]
    log_sig = jnp.minimum(z, 0.0) - jnp.log(1.0 + jnp.exp(-jnp.abs(z)))
    log_1mlb = jnp.log1p(-lb)
    logf = _logaddexp(jnp.log(lb), log_1mlb + log_sig) * LOG2E
    logk = (log_1mlb + log_sig - z) * LOG2E
    lhi = logf.astype(BF16)
    llo = (logf - lhi.astype(F32)).astype(BF16)
    tri = tri_ref[...]
    b = (jnp.dot(tri, lhi, preferred_element_type=F32)
         + jnp.dot(tri, llo, preferred_element_type=F32))
    lk = logk - b
    for h in range(HG_HEADS):
        cs = slice(h * LANES, (h + 1) * LANES)
        b_scr[h] = b[:, cs]
        lk_scr[h] = lk[:, cs]
        k_scr[h] = logk[:, cs]

    two = 2 * sub
    for h in range(HG_HEADS):
        cs = slice(h * LANES, (h + 1) * LANES)
        for r0 in range(0, c, two):
            bi = b[r0:r0 + two, cs]
            qi = q[r0:r0 + two, cs]
            for s in range(sub):
                def key_row(ref):
                    return jnp.concatenate([ref[h, pl.ds(r0 + s, sub, stride=0), :],
                                            ref[h, pl.ds(r0 + sub + s, sub, stride=0), :]], axis=0)
                d = qi * jnp.exp2(jnp.minimum(bi + key_row(lk_scr), key_row(k_scr)))
                d_scr[h, r0:r0 + two, s * LANES:(s + 1) * LANES] = d.astype(BF16)

    row = lax.broadcasted_iota(jnp.int32, (c, LANES), 0)
    col = lax.broadcasted_iota(jnp.int32, (c, LANES), 1)
    keep_diag = ((col // sub) == (row // sub)) & ((col % sub) <= (row % sub))
    half = c // 2
    rowq = lax.broadcasted_iota(jnp.int32, (half, LANES), 0)
    colq = lax.broadcasted_iota(jnp.int32, (half, LANES), 1)
    levels = []
    span = half
    while span >= sub:
        levels.append((span, ((colq // (2 * span)) == (rowq // span)) & ((colq % (2 * span)) < span)))
        span //= 2

    outs = []
    for h in range(HG_HEADS):
        cs = slice(h * LANES, (h + 1) * LANES)
        b_h, q_h, lk_h = b[:, cs], q[:, cs], lk[:, cs]
        v = hi_ref[:, cs]
        b_last = b_scr[h, c - 1:c, :]

        st = st_ref[h]
        o_h = lax.dot_general((q_h * jnp.exp2(b_h)).astype(BF16), st.astype(BF16), _NT,
                              preferred_element_type=F32)
        kh = jnp.exp2(lk_h + b_last).astype(BF16)
        st_ref[h] = st * jnp.exp2(b_last) + lax.dot_general(
            v, kh, _TN, preferred_element_type=F32)

        sdiag = jnp.where(keep_diag, jnp.dot(d_scr[h], esel_ref[...], preferred_element_type=F32), 0.0)
        pieces = [sdiag[i:i + 8] for i in range(0, c, 8)]

        for span, keep in levels:
            qparts, kparts = [], []
            for lo in range(0, c, 2 * span):
                mid, hi = lo + span, lo + 2 * span
                r = b_scr[h, mid - 1:mid, :]
                qparts.append(q_h[mid:hi] * jnp.exp2(b_h[mid:hi] - r))
                kparts.append(jnp.exp2(lk_h[lo:mid] + r))
                kparts.append(jnp.zeros((span, LANES), F32))
            sc = lax.dot_general(jnp.concatenate(qparts, axis=0).astype(BF16),
                                 jnp.concatenate(kparts, axis=0).astype(BF16), _NT,
                                 preferred_element_type=F32)
            sc = jnp.where(keep, sc, 0.0)
            for g, lo in enumerate(range(0, c, 2 * span)):
                for m in range(span // 8):
                    idx = (lo + span) // 8 + m
                    pieces[idx] = pieces[idx] + sc[g * span + 8 * m:g * span + 8 * m + 8]
        scores = jnp.concatenate(pieces, axis=0).astype(BF16)
        outs.append(o_h + jnp.dot(scores, v, preferred_element_type=F32))

    o = _rms(jnp.concatenate(outs, axis=1)) * onw_ref[...]
    g = hg_ref[...].astype(F32)
    o_ref[...] = (o * (g * jax.nn.sigmoid(g))).astype(o_ref.dtype)


def _hgrn(hq, hf, hi, hg, gamma, onw, *, layer):
    bsz, t, w = hq.shape
    c = HG_CHUNK
    tri = jnp.tril(jnp.ones((c, c), F32)).astype(BF16)
    sel = (jnp.arange(HG_SUB * LANES)[:, None] // LANES) == (jnp.arange(LANES)[None, :] % HG_SUB)
    esel = sel.astype(BF16)
    nb = HG_SEQS
    blk = pl.BlockSpec((nb, c, w), lambda b, i: (b, i, 0))
    return pl.pallas_call(
        functools.partial(_hgrn_body, layer=layer),
        grid=(bsz // nb, t // c),
        in_specs=[blk, blk, blk, blk,
                  _const_spec(gamma.shape), _const_spec(onw.shape),
                  _const_spec(tri.shape), _const_spec(esel.shape)],
        out_specs=blk,
        out_shape=jax.ShapeDtypeStruct((bsz, t, w), BF16),
        scratch_shapes=[
            pltpu.VMEM((nb, HG_HEADS, LANES, LANES), F32),
            pltpu.VMEM((nb, HG_HEADS, c, LANES), F32),
            pltpu.VMEM((nb, HG_HEADS, c, LANES), F32),
            pltpu.VMEM((nb, HG_HEADS, c, LANES), F32),
            pltpu.VMEM((nb, HG_HEADS, c, HG_SUB * LANES), BF16),
        ],
        compiler_params=pltpu.CompilerParams(
            dimension_semantics=("parallel", "arbitrary"), vmem_limit_bytes=VMEM_LIMIT),
        name="hgrn2",
    )(hq, hf, hi, hg, gamma, onw, tri, esel)


def _sb_body(q_ref, k_ref, v_ref, m_ref, nw_ref, o_ref, o_scr, c_scr, *, dh):
    blk = SB_BLOCK
    nh = LANES // dh
    n0 = nh * blk
    npairs = q_ref.shape[1] // LANES
    nwin = SB_LOOK + 1
    qi = pl.program_id(1)
    lane = lax.broadcasted_iota(jnp.int32, (blk, LANES), 1)
    row = lax.broadcasted_iota(jnp.int32, (n0, blk), 0)
    colk = lax.broadcasted_iota(jnp.int32, (n0, blk), 1)
    strict = colk < (row & (blk - 1))
    mcat = m_ref[...]

    def in_head(h):
        return (lane >= h * dh) & (lane < (h + 1) * dh)

    def head_rows(q):
        return jnp.concatenate(
            [jnp.where(in_head(h), q, jnp.zeros_like(q)) for h in range(nh)], axis=0)

    cmin = None
    for hp in range(npairs):
        cs = slice(hp * LANES, (hp + 1) * LANES)
        qs = head_rows(q_ref[:, cs])
        ks, vs = [], []
        for j in range(nwin):
            start = pl.multiple_of(jnp.maximum(qi - j, 0) * blk, blk)
            ks.append(k_ref[pl.ds(start, blk), cs])
            vs.append(v_ref[pl.ds(start, blk), cs])
        z = lax.dot_general(qs, jnp.concatenate(ks, axis=0), _NT, preferred_element_type=F32)
        z = jnp.concatenate([z[:, j * blk:(j + 1) * blk] for j in range(nwin)], axis=0)
        sp = _softplus(z)
        spm = jnp.concatenate([jnp.where(strict, sp[:n0], 0.0), sp[n0:]], axis=0)
        rt = jnp.dot(spm.astype(BF16), mcat, preferred_element_type=F32)
        e = z - sp - rt[:, :blk]
        c = jnp.zeros((n0, blk), F32)
        a_parts = []
        for j in range(nwin):
            sl = slice(j * n0, (j + 1) * n0)
            if j == 0:
                a = jnp.where(strict, jnp.exp(e[sl]), 0.0)
            else:
                a = jnp.exp(e[sl] - (c + jnp.where(qi >= j, 0.0, SB_OFF)))
            a_parts.append(a.astype(BF16))
            c = c + rt[sl, blk:]
        o_scr[hp] = jnp.dot(jnp.concatenate(a_parts, axis=1), jnp.concatenate(vs, axis=0),
                            preferred_element_type=F32)
        c_scr[hp] = c
        cmin = c if cmin is None else jnp.minimum(cmin, c)

    def more(carry):
        kb, cm = carry
        return (kb >= 0) & (cm < SB_DONE)

    def one_block(carry):
        kb, _ = carry
        start = pl.multiple_of(kb * blk, blk)
        cmin = None
        for hp in range(npairs):
            cs = slice(hp * LANES, (hp + 1) * LANES)
            qs = head_rows(q_ref[:, cs])
            z = lax.dot_general(qs, k_ref[pl.ds(start, blk), cs], _NT, preferred_element_type=F32)
            sp = _softplus(z)
            rt = jnp.dot(sp.astype(BF16), mcat, preferred_element_type=F32)
            c = c_scr[hp]
            a = jnp.exp(z - sp - rt[:, :blk] - c)
            o_scr[hp] += jnp.dot(a.astype(BF16), v_ref[pl.ds(start, blk), cs],
                                 preferred_element_type=F32)
            c = c + rt[:, blk:]
            c_scr[hp] = c
            cmin = c if cmin is None else jnp.minimum(cmin, c)
        return kb - 1, jnp.min(cmin)

    lax.while_loop(more, one_block, (qi - nwin, jnp.min(cmin)))

    parts = []
    for hp in range(npairs):
        o2 = o_scr[hp]
        res = o2[(nh - 1) * blk:]
        for h in range(nh - 1):
            res = jnp.where(in_head(h), o2[h * blk:(h + 1) * blk], res)
        parts.append(res)
    o = jnp.concatenate(parts, axis=1)
    o_ref[...] = (_rms(o) * nw_ref[...]).astype(o_ref.dtype)


def _stickbreak(sq, sk, sv, nw, *, dh):
    bsz, t, w = sq.shape
    blk = SB_BLOCK
    later = jnp.arange(blk)[:, None] > jnp.arange(blk)[None, :]
    mcat = jnp.concatenate([later, jnp.ones((blk, blk), bool)], axis=1).astype(BF16)
    qspec = pl.BlockSpec((None, blk, w), lambda b, i: (b, i, 0))
    kvspec = pl.BlockSpec((None, t, w), lambda b, i: (b, 0, 0))
    nh = LANES // dh
    return pl.pallas_call(
        functools.partial(_sb_body, dh=dh),
        grid=(bsz, t // blk),
        in_specs=[qspec, kvspec, kvspec, _const_spec(mcat.shape), _const_spec(nw.shape)],
        out_specs=qspec,
        out_shape=jax.ShapeDtypeStruct((bsz, t, w), BF16),
        scratch_shapes=[
            pltpu.VMEM((w // LANES, nh * blk, LANES), F32),
            pltpu.VMEM((w // LANES, nh * blk, blk), F32),
        ],
        compiler_params=pltpu.CompilerParams(
            dimension_semantics=("parallel", "arbitrary"), vmem_limit_bytes=VMEM_LIMIT),
        name="stickbreak",
    )(sq, sk, sv, mcat, nw)


def _tail_body(x_ref, ohg_ref, osb_ref, p_ref, apn_ref, fpre_ref, fpost_ref,
               wout_ref, wg_ref, wu_ref, wd_ref, pp_ref, pg_ref, o_ref, act_scr):
    mix_in = jnp.concatenate([ohg_ref[...], osb_ref[...]], axis=-1)
    mix = jnp.dot(mix_in, wout_ref[...], preferred_element_type=F32)
    h1 = x_ref[...] + _rms(mix) * apn_ref[...]
    u = (_rms(h1) * fpre_ref[...]).astype(BF16)

    ff = act_scr.shape[1] // FF_SPLIT
    for j in range(FF_SPLIT):
        fs = slice(j * ff, (j + 1) * ff)
        g = jnp.dot(u, wg_ref[:, fs], preferred_element_type=F32)
        up = jnp.dot(u, wu_ref[:, fs], preferred_element_type=F32)
        act_scr[:, fs] = (g * jax.nn.sigmoid(g) * up).astype(BF16)
    y = jnp.dot(act_scr[...], wd_ref[...], preferred_element_type=F32)
    h2 = h1 + _rms(y) * fpost_ref[...]
    emb = jnp.dot(p_ref[...].astype(BF16), pp_ref[...], preferred_element_type=F32)
    gate = jnp.dot(h2.astype(BF16), pg_ref[...], preferred_element_type=F32)
    o_ref[...] = h2 + emb * jax.nn.sigmoid(gate)


def _tail(x2, ohg, osb, p2, apn, fpre, fpost, wout, wg, wu, wd, pp, pg, *, tm):
    n, d = x2.shape
    row = lambda a: pl.BlockSpec((tm, a.shape[1]), lambda i: (i, 0))
    consts = (apn, fpre, fpost, wout, wg, wu, wd, pp, pg)
    return pl.pallas_call(
        _tail_body,
        grid=(n // tm,),
        in_specs=[row(x2), row(ohg), row(osb), row(p2)] + [_const_spec(a.shape) for a in consts],
        out_specs=row(x2),
        out_shape=jax.ShapeDtypeStruct((n, d), F32),
        scratch_shapes=[pltpu.VMEM((tm, wd.shape[0]), BF16)],
        compiler_params=pltpu.CompilerParams(
            dimension_semantics=("parallel",), vmem_limit_bytes=VMEM_LIMIT),
        name="tail",
    )(x2, ohg, osb, p2, *consts)


def kernel(x, p, attn_pre_norm, w_in, hg_lower_gamma, hg_out_norm, sb_out_norm, w_out, attn_post_norm, ffn_pre_norm, w_gate_up, w_down, ffn_post_norm, ple_proj, ple_gate):
    bsz, t, d = x.shape
    depth = w_in.shape[0]
    n = bsz * t
    width = hg_out_norm.shape[1]
    assert sb_out_norm.shape[1] == width and w_in.shape[2] == 7 * width
    assert width == HG_HEADS * LANES and t % HG_CHUNK == 0 and t % SB_BLOCK == 0
    assert bsz % HG_SEQS == 0
    dh = width // SB_HEADS
    d_ff = w_down.shape[1]
    assert d_ff % (FF_SPLIT * LANES) == 0
    tm = 512
    assert n % tm == 0

    h = x.reshape(n, d)
    for i in range(depth):
        hq, hf, hi, hg, sq, sk, sv = _inproj(
            h, attn_pre_norm[i][None], w_in[i].astype(BF16),
            width=width, sb_scale=dh ** -0.5, tm=tm)
        to3 = lambda a: a.reshape(bsz, t, width)
        o_hg = _hgrn(to3(hq), to3(hf), to3(hi), to3(hg), hg_lower_gamma,
                     hg_out_norm[i][None], layer=i)
        o_sb = _stickbreak(to3(sq), to3(sk), to3(sv), sb_out_norm[i][None], dh=dh)
        wgu = w_gate_up[i].astype(BF16)
        h = _tail(
            h, o_hg.reshape(n, width), o_sb.reshape(n, width), p[i].reshape(n, -1),
            attn_post_norm[i][None], ffn_pre_norm[i][None],
            ffn_post_norm[i][None], w_out[i].astype(BF16),
            wgu[:, :d_ff], wgu[:, d_ff:], w_down[i].astype(BF16),
            ple_proj[i].astype(BF16), ple_gate[i].astype(BF16), tm=tm)
    return h.reshape(bsz, t, d)
```

```python
import functools

import jax
import jax.numpy as jnp
from jax import lax
from jax.experimental import pallas as pl
from jax.experimental.pallas import tpu as pltpu

F32 = jnp.float32
BF16 = jnp.bfloat16
EPS = 1e-6
LOG2E = 1.4426950408889634

HG_HEADS = 4
SB_HEADS = 8
LANES = 128
HG_CHUNK = 128
HG_SUB = 8
HG_SEQS = 2
SB_BLOCK = 128
SB_LOOK = 2
SB_DONE = 115.0
SB_OFF = 1e30
FF_SPLIT = 2
VMEM_LIMIT = 56 * 1024 * 1024

_NT = (((1,), (1,)), ((), ()))
_TN = (((0,), (0,)), ((), ()))


def _rms(xf):
    return xf * lax.rsqrt(jnp.mean(xf * xf, axis=-1, keepdims=True) + EPS)


def _softplus(z):
    return jnp.maximum(z, 0.0) + jnp.log(1.0 + jnp.exp2(jnp.abs(z) * -LOG2E))


def _logaddexp(a, c):
    return jnp.maximum(a, c) + jnp.log(1.0 + jnp.exp(-jnp.abs(a - c)))


def _const_spec(shape):
    nd = len(shape)
    return pl.BlockSpec(shape, lambda *_: (0,) * nd, pipeline_mode=pl.Buffered(1))


def _inproj_body(x_ref, nw_ref, w_ref, hq_ref, hf_ref, hi_ref, hg_ref, sq_ref, sk_ref, sv_ref,
                 *, width, sb_scale):
    u = (_rms(x_ref[...]) * nw_ref[...]).astype(BF16)
    outs = (hq_ref, hf_ref, hi_ref, hg_ref, sq_ref, sk_ref, sv_ref)
    for j, o_ref in enumerate(outs):
        r = jnp.dot(u, w_ref[:, j * width:(j + 1) * width], preferred_element_type=F32)
        if o_ref is sq_ref:
            r = r * sb_scale
        o_ref[...] = r.astype(o_ref.dtype)


def _inproj(x2, nw, w_in, *, width, sb_scale, tm):
    n, d = x2.shape
    out_dtypes = (BF16, F32, BF16, BF16, BF16, BF16, BF16)
    return pl.pallas_call(
        functools.partial(_inproj_body, width=width, sb_scale=sb_scale),
        grid=(n // tm,),
        in_specs=[
            pl.BlockSpec((tm, d), lambda i: (i, 0)),
            _const_spec((1, d)),
            _const_spec(w_in.shape),
        ],
        out_specs=[pl.BlockSpec((tm, width), lambda i: (i, 0)) for _ in out_dtypes],
        out_shape=[jax.ShapeDtypeStruct((n, width), dt) for dt in out_dtypes],
        compiler_params=pltpu.CompilerParams(
            dimension_semantics=("parallel",), vmem_limit_bytes=VMEM_LIMIT),
        name="inproj",
    )(x2, nw, w_in)


def _hgrn_body(hq_ref, hf_ref, hi_ref, hg_ref, gam_ref, onw_ref, tri_ref, esel_ref, o_ref,
               st_ref, b_scr, lk_scr, k_scr, d_scr, *, layer):
    @pl.when(pl.program_id(1) == 0)
    def _():
        st_ref[...] = jnp.zeros_like(st_ref)

    for i in range(hq_ref.shape[0]):
        _hgrn_chunk(hq_ref.at[i], hf_ref.at[i], hi_ref.at[i], hg_ref.at[i], gam_ref, onw_ref,
                    tri_ref, esel_ref, o_ref.at[i], st_ref.at[i], b_scr.at[i], lk_scr.at[i],
                    k_scr.at[i], d_scr.at[i], layer=layer)


def _hgrn_chunk(hq_ref, hf_ref, hi_ref, hg_ref, gam_ref, onw_ref, tri_ref, esel_ref, o_ref,
                st_ref, b_scr, lk_scr, k_scr, d_scr, *, layer):
    c = HG_CHUNK
    sub = HG_SUB

    gam = gam_ref[...]
    e = jnp.exp(gam - jnp.max(gam, axis=0, keepdims=True))
    lb = jnp.sum(e[:layer + 1], axis=0, keepdims=True) / jnp.sum(e, axis=0, keepdims=True)

    qz = hq_ref[...].astype(F32)
    q = qz * jax.nn.sigmoid(qz)
    z = hf_ref[...]
    log_sig = jnp.minimum(z, 0.0) - jnp.log(1.0 + jnp.exp(-jnp.abs(z)))
    log_1mlb = jnp.log1p(-lb)
    logf = _logaddexp(jnp.log(lb), log_1mlb + log_sig) * LOG2E
    logk = (log_1mlb + log_sig - z) * LOG2E
    lhi = logf.astype(BF16)
    llo = (logf - lhi.astype(F32)).astype(BF16)
    tri = tri_ref[...]
    b = (jnp.dot(tri, lhi, preferred_element_type=F32)
         + jnp.dot(tri, llo, preferred_element_type=F32))
    lk = logk - b
    for h in range(HG_HEADS):
        cs = slice(h * LANES, (h + 1) * LANES)
        b_scr[h] = b[:, cs]
        lk_scr[h] = lk[:, cs]
        k_scr[h] = logk[:, cs]

    two = 2 * sub
    for h in range(HG_HEADS):
        cs = slice(h * LANES, (h + 1) * LANES)
        for r0 in range(0, c, two):
            bi = b[r0:r0 + two, cs]
            qi = q[r0:r0 + two, cs]
            for s in range(sub):
                def key_row(ref):
                    return jnp.concatenate([ref[h, pl.ds(r0 + s, sub, stride=0), :],
                                            ref[h, pl.ds(r0 + sub + s, sub, stride=0), :]], axis=0)
                d = qi * jnp.exp2(jnp.minimum(bi + key_row(lk_scr), key_row(k_scr)))
                d_scr[h, r0:r0 + two, s * LANES:(s + 1) * LANES] = d.astype(BF16)

    row = lax.broadcasted_iota(jnp.int32, (c, LANES), 0)
    col = lax.broadcasted_iota(jnp.int32, (c, LANES), 1)
    keep_diag = ((col // sub) == (row // sub)) & ((col % sub) <= (row % sub))
    half = c // 2
    rowq = lax.broadcasted_iota(jnp.int32, (half, LANES), 0)
    colq = lax.broadcasted_iota(jnp.int32, (half, LANES), 1)
    levels = []
    span = half
    while span >= sub:
        levels.append((span, ((colq // (2 * span)) == (rowq // span)) & ((colq % (2 * span)) < span)))
        span //= 2

    outs = []
    for h in range(HG_HEADS):
        cs = slice(h * LANES, (h + 1) * LANES)
        b_h, q_h, lk_h = b[:, cs], q[:, cs], lk[:, cs]
        v = hi_ref[:, cs]
        b_last = b_scr[h, c - 1:c, :]

        st = st_ref[h]
        o_h = lax.dot_general((q_h * jnp.exp2(b_h)).astype(BF16), st.astype(BF16), _NT,
                              preferred_element_type=F32)
        kh = jnp.exp2(lk_h + b_last).astype(BF16)
        st_ref[h] = st * jnp.exp2(b_last) + lax.dot_general(
            v, kh, _TN, preferred_element_type=F32)

        sdiag = jnp.where(keep_diag, jnp.dot(d_scr[h], esel_ref[...], preferred_element_type=F32), 0.0)
        pieces = [sdiag[i:i + 8] for i in range(0, c, 8)]

        for span, keep in levels:
            qparts, kparts = [], []
            for lo in range(0, c, 2 * span):
                mid, hi = lo + span, lo + 2 * span
                r = b_scr[h, mid - 1:mid, :]
                qparts.append(q_h[mid:hi] * jnp.exp2(b_h[mid:hi] - r))
                kparts.append(jnp.exp2(lk_h[lo:mid] + r))
                kparts.append(jnp.zeros((span, LANES), F32))
            sc = lax.dot_general(jnp.concatenate(qparts, axis=0).astype(BF16),
                                 jnp.concatenate(kparts, axis=0).astype(BF16), _NT,
                                 preferred_element_type=F32)
            sc = jnp.where(keep, sc, 0.0)
            for g, lo in enumerate(range(0, c, 2 * span)):
                for m in range(span // 8):
                    idx = (lo + span) // 8 + m
                    pieces[idx] = pieces[idx] + sc[g * span + 8 * m:g * span + 8 * m + 8]
        scores = jnp.concatenate(pieces, axis=0).astype(BF16)
        outs.append(o_h + jnp.dot(scores, v, preferred_element_type=F32))

    o = _rms(jnp.concatenate(outs, axis=1)) * onw_ref[...]
    g = hg_ref[...].astype(F32)
    o_ref[...] = (o * (g * jax.nn.sigmoid(g))).astype(o_ref.dtype)


def _hgrn(hq, hf, hi, hg, gamma, onw, *, layer):
    bsz, t, w = hq.shape
    c = HG_CHUNK
    tri = jnp.tril(jnp.ones((c, c), F32)).astype(BF16)
    sel = (jnp.arange(HG_SUB * LANES)[:, None] // LANES) == (jnp.arange(LANES)[None, :] % HG_SUB)
    esel = sel.astype(BF16)
    nb = HG_SEQS
    blk = pl.BlockSpec((nb, c, w), lambda b, i: (b, i, 0))
    return pl.pallas_call(
        functools.partial(_hgrn_body, layer=layer),
        grid=(bsz // nb, t // c),
        in_specs=[blk, blk, blk, blk,
                  _const_spec(gamma.shape), _const_spec(onw.shape),
                  _const_spec(tri.shape), _const_spec(esel.shape)],
        out_specs=blk,
        out_shape=jax.ShapeDtypeStruct((bsz, t, w), BF16),
        scratch_shapes=[
            pltpu.VMEM((nb, HG_HEADS, LANES, LANES), F32),
            pltpu.VMEM((nb, HG_HEADS, c, LANES), F32),
            pltpu.VMEM((nb, HG_HEADS, c, LANES), F32),
            pltpu.VMEM((nb, HG_HEADS, c, LANES), F32),
            pltpu.VMEM((nb, HG_HEADS, c, HG_SUB * LANES), BF16),
        ],
        compiler_params=pltpu.CompilerParams(
            dimension_semantics=("parallel", "arbitrary"), vmem_limit_bytes=VMEM_LIMIT),
        name="hgrn2",
    )(hq, hf, hi, hg, gamma, onw, tri, esel)


def _sb_body(q_ref, k_ref, v_ref, m_ref, nw_ref, o_ref, o_scr, c_scr, *, dh):
    blk = SB_BLOCK
    nh = LANES // dh
    n0 = nh * blk
    npairs = q_ref.shape[1] // LANES
    nwin = SB_LOOK + 1
    qi = pl.program_id(1)
    lane = lax.broadcasted_iota(jnp.int32, (blk, LANES), 1)
    row = lax.broadcasted_iota(jnp.int32, (n0, blk), 0)
    colk = lax.broadcasted_iota(jnp.int32, (n0, blk), 1)
    strict = colk < (row & (blk - 1))
    mcat = m_ref[...]

    def in_head(h):
        return (lane >= h * dh) & (lane < (h + 1) * dh)

    def head_rows(q):
        return jnp.concatenate(
            [jnp.where(in_head(h), q, jnp.zeros_like(q)) for h in range(nh)], axis=0)

    css = [slice(hp * LANES, (hp + 1) * LANES) for hp in range(npairs)]
    starts = [pl.multiple_of(jnp.maximum(qi - j, 0) * blk, blk) for j in range(nwin)]
    zs = []
    for cs in css:
        kwin = jnp.concatenate([k_ref[pl.ds(st, blk), cs] for st in starts], axis=0)
        z = lax.dot_general(head_rows(q_ref[:, cs]), kwin, _NT, preferred_element_type=F32)
        zs.append(jnp.concatenate([z[:, j * blk:(j + 1) * blk] for j in range(nwin)], axis=0))
    sps = [_softplus(z) for z in zs]
    rts = [jnp.dot(jnp.concatenate([jnp.where(strict, sp[:n0], 0.0), sp[n0:]], axis=0).astype(BF16),
                   mcat, preferred_element_type=F32) for sp in sps]
    cmin = None
    for hp, cs in enumerate(css):
        z, sp, rt = zs[hp], sps[hp], rts[hp]
        e = z - sp - rt[:, :blk]
        c = jnp.zeros((n0, blk), F32)
        a_parts = []
        for j in range(nwin):
            sl = slice(j * n0, (j + 1) * n0)
            if j == 0:
                a = jnp.where(strict, jnp.exp(e[sl]), 0.0)
            else:
                a = jnp.exp(e[sl] - (c + jnp.where(qi >= j, 0.0, SB_OFF)))
            a_parts.append(a.astype(BF16))
            c = c + rt[sl, blk:]
        vwin = jnp.concatenate([v_ref[pl.ds(st, blk), cs] for st in starts], axis=0)
        o_scr[hp] = jnp.dot(jnp.concatenate(a_parts, axis=1), vwin, preferred_element_type=F32)
        c_scr[hp] = c
        cmin = c if cmin is None else jnp.minimum(cmin, c)

    def more(carry):
        kb, cm = carry
        return (kb >= 0) & (cm < SB_DONE)

    def one_block(carry):
        kb, _ = carry
        start = pl.multiple_of(kb * blk, blk)
        cmin = None
        for hp in range(npairs):
            cs = slice(hp * LANES, (hp + 1) * LANES)
            qs = head_rows(q_ref[:, cs])
            z = lax.dot_general(qs, k_ref[pl.ds(start, blk), cs], _NT, preferred_element_type=F32)
            sp = _softplus(z)
            rt = jnp.dot(sp.astype(BF16), mcat, preferred_element_type=F32)
            c = c_scr[hp]
            a = jnp.exp(z - sp - rt[:, :blk] - c)
            o_scr[hp] += jnp.dot(a.astype(BF16), v_ref[pl.ds(start, blk), cs],
                                 preferred_element_type=F32)
            c = c + rt[:, blk:]
            c_scr[hp] = c
            cmin = c if cmin is None else jnp.minimum(cmin, c)
        return kb - 1, jnp.min(cmin)

    lax.while_loop(more, one_block, (qi - nwin, jnp.min(cmin)))

    parts = []
    for hp in range(npairs):
        o2 = o_scr[hp]
        res = o2[(nh - 1) * blk:]
        for h in range(nh - 1):
            res = jnp.where(in_head(h), o2[h * blk:(h + 1) * blk], res)
        parts.append(res)
    o = jnp.concatenate(parts, axis=1)
    o_ref[...] = (_rms(o) * nw_ref[...]).astype(o_ref.dtype)


def _stickbreak(sq, sk, sv, nw, *, dh):
    bsz, t, w = sq.shape
    blk = SB_BLOCK
    later = jnp.arange(blk)[:, None] > jnp.arange(blk)[None, :]
    mcat = jnp.concatenate([later, jnp.ones((blk, blk), bool)], axis=1).astype(BF16)
    qspec = pl.BlockSpec((None, blk, w), lambda b, i: (b, i, 0))
    kvspec = pl.BlockSpec((None, t, w), lambda b, i: (b, 0, 0))
    nh = LANES // dh
    return pl.pallas_call(
        functools.partial(_sb_body, dh=dh),
        grid=(bsz, t // blk),
        in_specs=[qspec, kvspec, kvspec, _const_spec(mcat.shape), _const_spec(nw.shape)],
        out_specs=qspec,
        out_shape=jax.ShapeDtypeStruct((bsz, t, w), BF16),
        scratch_shapes=[
            pltpu.VMEM((w // LANES, nh * blk, LANES), F32),
            pltpu.VMEM((w // LANES, nh * blk, blk), F32),
        ],
        compiler_params=pltpu.CompilerParams(
            dimension_semantics=("parallel", "arbitrary"), vmem_limit_bytes=VMEM_LIMIT),
        name="stickbreak",
    )(sq, sk, sv, mcat, nw)


def _tail_body(x_ref, ohg_ref, osb_ref, p_ref, apn_ref, fpre_ref, fpost_ref,
               wout_ref, wg_ref, wu_ref, wd_ref, pp_ref, pg_ref, o_ref, act_scr):
    mix_in = jnp.concatenate([ohg_ref[...], osb_ref[...]], axis=-1)
    mix = jnp.dot(mix_in, wout_ref[...], preferred_element_type=F32)
    h1 = x_ref[...] + _rms(mix) * apn_ref[...]
    u = (_rms(h1) * fpre_ref[...]).astype(BF16)

    ff = act_scr.shape[1] // FF_SPLIT
    for j in range(FF_SPLIT):
        fs = slice(j * ff, (j + 1) * ff)
        g = jnp.dot(u, wg_ref[:, fs], preferred_element_type=F32)
        up = jnp.dot(u, wu_ref[:, fs], preferred_element_type=F32)
        act_scr[:, fs] = (g * jax.nn.sigmoid(g) * up).astype(BF16)
    y = jnp.dot(act_scr[...], wd_ref[...], preferred_element_type=F32)
    h2 = h1 + _rms(y) * fpost_ref[...]
    emb = jnp.dot(p_ref[...].astype(BF16), pp_ref[...], preferred_element_type=F32)
    gate = jnp.dot(h2.astype(BF16), pg_ref[...], preferred_element_type=F32)
    o_ref[...] = h2 + emb * jax.nn.sigmoid(gate)


def _tail(x2, ohg, osb, p2, apn, fpre, fpost, wout, wg, wu, wd, pp, pg, *, tm):
    n, d = x2.shape
    row = lambda a: pl.BlockSpec((tm, a.shape[1]), lambda i: (i, 0))
    consts = (apn, fpre, fpost, wout, wg, wu, wd, pp, pg)
    return pl.pallas_call(
        _tail_body,
        grid=(n // tm,),
        in_specs=[row(x2), row(ohg), row(osb), row(p2)] + [_const_spec(a.shape) for a in consts],
        out_specs=row(x2),
        out_shape=jax.ShapeDtypeStruct((n, d), F32),
        scratch_shapes=[pltpu.VMEM((tm, wd.shape[0]), BF16)],
        compiler_params=pltpu.CompilerParams(
            dimension_semantics=("parallel",), vmem_limit_bytes=VMEM_LIMIT),
        name="tail",
    )(x2, ohg, osb, p2, *consts)


def kernel(x, p, attn_pre_norm, w_in, hg_lower_gamma, hg_out_norm, sb_out_norm, w_out, attn_post_norm, ffn_pre_norm, w_gate_up, w_down, ffn_post_norm, ple_proj, ple_gate):
    bsz, t, d = x.shape
    depth = w_in.shape[0]
    n = bsz * t
    width = hg_out_norm.shape[1]
    assert sb_out_norm.shape[1] == width and w_in.shape[2] == 7 * width
    assert width == HG_HEADS * LANES and t % HG_CHUNK == 0 and t % SB_BLOCK == 0
    assert bsz % HG_SEQS == 0
    dh = width // SB_HEADS
    d_ff = w_down.shape[1]
    assert d_ff % (FF_SPLIT * LANES) == 0
    tm = 512
    assert n % tm == 0

    h = x.reshape(n, d)
    for i in range(depth):
        hq, hf, hi, hg, sq, sk, sv = _inproj(
            h, attn_pre_norm[i][None], w_in[i].astype(BF16),
            width=width, sb_scale=dh ** -0.5, tm=tm)
        to3 = lambda a: a.reshape(bsz, t, width)
        o_hg = _hgrn(to3(hq), to3(hf), to3(hi), to3(hg), hg_lower_gamma,
                     hg_out_norm[i][None], layer=i)
        o_sb = _stickbreak(to3(sq), to3(sk), to3(sv), sb_out_norm[i][None], dh=dh)
        wgu = w_gate_up[i].astype(BF16)
        h = _tail(
            h, o_hg.reshape(n, width), o_sb.reshape(n, width), p[i].reshape(n, -1),
            attn_post_norm[i][None], ffn_pre_norm[i][None],
            ffn_post_norm[i][None], w_out[i].astype(BF16),
            wgu[:, :d_ff], wgu[:, d_ff:], w_down[i].astype(BF16),
            ple_proj[i].astype(BF16), ple_gate[i].astype(BF16), tm=tm)
    return h.reshape(bsz, t, d)
```

```python
import functools

import jax
import jax.numpy as jnp
from jax import lax
from jax.experimental import pallas as pl
from jax.experimental.pallas import tpu as pltpu

F32 = jnp.float32
BF16 = jnp.bfloat16
EPS = 1e-6
LOG2E = 1.4426950408889634

HG_HEADS = 4
SB_HEADS = 8
LANES = 128
HG_CHUNK = 128
HG_SUB = 8
SB_BLOCK = 128
SB_LOOK = 2
SB_DONE = 115.0
SB_OFF = 1e30
FF_SLAB = 256
TAIL_SLABS = 4
VMEM_LIMIT = 56 * 1024 * 1024

_NT = (((1,), (1,)), ((), ()))
_TN = (((0,), (0,)), ((), ()))


def _rms(xf):
    return xf * lax.rsqrt(jnp.mean(xf * xf, axis=-1, keepdims=True) + EPS)


def _softplus(z):
    return jnp.maximum(z, 0.0) + jnp.log(1.0 + jnp.exp2(jnp.abs(z) * -LOG2E))


def _logaddexp(a, c):
    return jnp.maximum(a, c) + jnp.log(1.0 + jnp.exp(-jnp.abs(a - c)))


def _const_spec(shape):
    nd = len(shape)
    return pl.BlockSpec(shape, lambda *_: (0,) * nd, pipeline_mode=pl.Buffered(1))


def _inproj_body(x_ref, nw_ref, w_ref, hq_ref, hf_ref, hi_ref, hg_ref, sq_ref, sk_ref, sv_ref,
                 *, width, sb_scale):
    u = (_rms(x_ref[...]) * nw_ref[...]).astype(BF16)
    outs = (hq_ref, hf_ref, hi_ref, hg_ref, sq_ref, sk_ref, sv_ref)
    for j, o_ref in enumerate(outs):
        r = jnp.dot(u, w_ref[:, j * width:(j + 1) * width], preferred_element_type=F32)
        if o_ref is sq_ref:
            r = r * sb_scale
        o_ref[...] = r.astype(o_ref.dtype)


def _inproj(x2, nw, w_in, *, width, sb_scale, tm):
    n, d = x2.shape
    out_dtypes = (BF16, F32, BF16, BF16, BF16, BF16, BF16)
    return pl.pallas_call(
        functools.partial(_inproj_body, width=width, sb_scale=sb_scale),
        grid=(n // tm,),
        in_specs=[
            pl.BlockSpec((tm, d), lambda i: (i, 0)),
            _const_spec((1, d)),
            _const_spec(w_in.shape),
        ],
        out_specs=[pl.BlockSpec((tm, width), lambda i: (i, 0)) for _ in out_dtypes],
        out_shape=[jax.ShapeDtypeStruct((n, width), dt) for dt in out_dtypes],
        compiler_params=pltpu.CompilerParams(
            dimension_semantics=("parallel",), vmem_limit_bytes=VMEM_LIMIT),
        name="inproj",
    )(x2, nw, w_in)


def _hgrn_stages(hq_ref, hf_ref, hi_ref, lb, tri_ref, esel_ref, st_ref, b_scr, lk_scr, k_scr, d_scr,
                 emit):
    c = HG_CHUNK
    sub = HG_SUB

    qz = hq_ref[...].astype(F32)
    q = qz * jax.nn.sigmoid(qz)
    z = hf_ref[...]
    log_sig = jnp.minimum(z, 0.0) - jnp.log(1.0 + jnp.exp(-jnp.abs(z)))
    log_1mlb = jnp.log1p(-lb)
    logf = _logaddexp(jnp.log(lb), log_1mlb + log_sig) * LOG2E
    logk = (log_1mlb + log_sig - z) * LOG2E
    lhi = logf.astype(BF16)
    llo = (logf - lhi.astype(F32)).astype(BF16)
    tri = tri_ref[...]
    b = (jnp.dot(tri, lhi, preferred_element_type=F32)
         + jnp.dot(tri, llo, preferred_element_type=F32))
    lk = logk - b
    for h in range(HG_HEADS):
        cs = slice(h * LANES, (h + 1) * LANES)
        b_scr[h] = b[:, cs]
        lk_scr[h] = lk[:, cs]
        k_scr[h] = logk[:, cs]
    yield

    two = 2 * sub
    for h in range(HG_HEADS):
        cs = slice(h * LANES, (h + 1) * LANES)
        for r0 in range(0, c, two):
            bi = b[r0:r0 + two, cs]
            qi = q[r0:r0 + two, cs]
            for s in range(sub):
                def key_row(ref):
                    return jnp.concatenate([ref[h, pl.ds(r0 + s, sub, stride=0), :],
                                            ref[h, pl.ds(r0 + sub + s, sub, stride=0), :]], axis=0)
                d = qi * jnp.exp2(jnp.minimum(bi + key_row(lk_scr), key_row(k_scr)))
                d_scr[h, r0:r0 + two, s * LANES:(s + 1) * LANES] = d.astype(BF16)
        yield

    row = lax.broadcasted_iota(jnp.int32, (c, LANES), 0)
    col = lax.broadcasted_iota(jnp.int32, (c, LANES), 1)
    keep_diag = ((col // sub) == (row // sub)) & ((col % sub) <= (row % sub))
    half = c // 2
    rowq = lax.broadcasted_iota(jnp.int32, (half, LANES), 0)
    colq = lax.broadcasted_iota(jnp.int32, (half, LANES), 1)
    levels = []
    span = half
    while span >= sub:
        levels.append((span, ((colq // (2 * span)) == (rowq // span)) & ((colq % (2 * span)) < span)))
        span //= 2

    outs = []
    for h in range(HG_HEADS):
        cs = slice(h * LANES, (h + 1) * LANES)
        b_h, q_h, lk_h = b[:, cs], q[:, cs], lk[:, cs]
        v = hi_ref[:, cs]
        b_last = b_scr[h, c - 1:c, :]

        st = st_ref[h]
        o_h = lax.dot_general((q_h * jnp.exp2(b_h)).astype(BF16), st.astype(BF16), _NT,
                              preferred_element_type=F32)
        kh = jnp.exp2(lk_h + b_last).astype(BF16)
        st_ref[h] = st * jnp.exp2(b_last) + lax.dot_general(
            v, kh, _TN, preferred_element_type=F32)

        sdiag = jnp.where(keep_diag, jnp.dot(d_scr[h], esel_ref[...], preferred_element_type=F32), 0.0)
        pieces = [sdiag[i:i + 8] for i in range(0, c, 8)]

        for span, keep in levels:
            qparts, kparts = [], []
            for lo in range(0, c, 2 * span):
                mid, hi = lo + span, lo + 2 * span
                r = b_scr[h, mid - 1:mid, :]
                qparts.append(q_h[mid:hi] * jnp.exp2(b_h[mid:hi] - r))
                kparts.append(jnp.exp2(lk_h[lo:mid] + r))
                kparts.append(jnp.zeros((span, LANES), F32))
            sc = lax.dot_general(jnp.concatenate(qparts, axis=0).astype(BF16),
                                 jnp.concatenate(kparts, axis=0).astype(BF16), _NT,
                                 preferred_element_type=F32)
            sc = jnp.where(keep, sc, 0.0)
            for g, lo in enumerate(range(0, c, 2 * span)):
                for m in range(span // 8):
                    idx = (lo + span) // 8 + m
                    pieces[idx] = pieces[idx] + sc[g * span + 8 * m:g * span + 8 * m + 8]
        scores = jnp.concatenate(pieces, axis=0).astype(BF16)
        outs.append(o_h + jnp.dot(scores, v, preferred_element_type=F32))
        yield
    emit(jnp.concatenate(outs, axis=1))


def _sb_masks(dh):
    blk = SB_BLOCK
    n0 = (LANES // dh) * blk
    lane = lax.broadcasted_iota(jnp.int32, (blk, LANES), 1)
    row = lax.broadcasted_iota(jnp.int32, (n0, blk), 0)
    colk = lax.broadcasted_iota(jnp.int32, (n0, blk), 1)
    strict = colk < (row & (blk - 1))
    in_head = [(lane >= h * dh) & (lane < (h + 1) * dh) for h in range(LANES // dh)]
    return strict, in_head


def _head_rows(q, in_head):
    return jnp.concatenate([jnp.where(m, q, jnp.zeros_like(q)) for m in in_head], axis=0)


def _sb_stages(q_refs, qis, k_ref, v_ref, mcat, strict, in_head, o_scrs, c_scrs, emit):
    blk = SB_BLOCK
    n0 = len(in_head) * blk
    nwin = SB_LOOK + 1
    npairs = q_refs[0].shape[1] // LANES
    chains = [(r, hp) for r in range(len(q_refs)) for hp in range(npairs)]
    starts = [[pl.multiple_of(jnp.maximum(qi - j, 0) * blk, blk) for j in range(nwin)] for qi in qis]
    zs = []
    for n, (r, hp) in enumerate(chains):
        cs = slice(hp * LANES, (hp + 1) * LANES)
        kwin = jnp.concatenate([k_ref[pl.ds(st, blk), cs] for st in starts[r]], axis=0)
        z = lax.dot_general(_head_rows(q_refs[r][:, cs], in_head), kwin, _NT,
                            preferred_element_type=F32)
        zs.append(jnp.concatenate([z[:, j * blk:(j + 1) * blk] for j in range(nwin)], axis=0))
        if n % 2:
            yield
    sps = []
    for n, z in enumerate(zs):
        sps.append(_softplus(z))
        if n % 2:
            yield
    rts = []
    for n, sp in enumerate(sps):
        spm = jnp.concatenate([jnp.where(strict, sp[:n0], 0.0), sp[n0:]], axis=0)
        rts.append(jnp.dot(spm.astype(BF16), mcat, preferred_element_type=F32))
        if n % 2:
            yield
    cmin = None
    for n, (r, hp) in enumerate(chains):
        cs = slice(hp * LANES, (hp + 1) * LANES)
        z, sp, rt = zs[n], sps[n], rts[n]
        e = z - sp - rt[:, :blk]
        c = jnp.zeros((n0, blk), F32)
        a_parts = []
        for j in range(nwin):
            sl = slice(j * n0, (j + 1) * n0)
            if j == 0:
                a = jnp.where(strict, jnp.exp(e[sl]), 0.0)
            else:
                a = jnp.exp(e[sl] - (c + jnp.where(qis[r] >= j, 0.0, SB_OFF)))
            a_parts.append(a.astype(BF16))
            c = c + rt[sl, blk:]
        vwin = jnp.concatenate([v_ref[pl.ds(st, blk), cs] for st in starts[r]], axis=0)
        o_scrs[r][hp] = jnp.dot(jnp.concatenate(a_parts, axis=1), vwin, preferred_element_type=F32)
        c_scrs[r][hp] = c
        cmin = c if cmin is None else jnp.minimum(cmin, c)
        yield
    emit(cmin)


def _sb_rest(q_refs, qis, k_ref, v_ref, mcat, in_head, o_scrs, c_scrs, cmin):
    blk = SB_BLOCK
    nwin = SB_LOOK + 1
    npairs = q_refs[0].shape[1] // LANES
    top = qis[-1] - nwin

    def more(carry):
        d, cm = carry
        return (top - d >= 0) & (cm < SB_DONE)

    def one_block(carry):
        d, _ = carry
        cmin = None
        for r, qi in enumerate(qis):
            kb = qi - nwin - d
            live = kb >= 0
            start = pl.multiple_of(jnp.maximum(kb, 0) * blk, blk)
            off = jnp.where(live, 0.0, SB_OFF)
            keep = jnp.where(live, 1.0, 0.0)
            for hp in range(npairs):
                cs = slice(hp * LANES, (hp + 1) * LANES)
                qs = _head_rows(q_refs[r][:, cs], in_head)
                z = lax.dot_general(qs, k_ref[pl.ds(start, blk), cs], _NT, preferred_element_type=F32)
                sp = _softplus(z)
                rt = jnp.dot(sp.astype(BF16), mcat, preferred_element_type=F32)
                c = c_scrs[r][hp]
                a = jnp.exp(z - sp - rt[:, :blk] - (c + off))
                o_scrs[r][hp] += jnp.dot(a.astype(BF16), v_ref[pl.ds(start, blk), cs],
                                         preferred_element_type=F32)
                c = c + rt[:, blk:] * keep
                c_scrs[r][hp] = c
                cmin = c if cmin is None else jnp.minimum(cmin, c)
        return d + 1, jnp.min(cmin)

    lax.while_loop(more, one_block, (jnp.int32(0), jnp.min(cmin)))


def _sb_finish(o_scr, in_head):
    blk = SB_BLOCK
    nh = len(in_head)
    parts = []
    for hp in range(o_scr.shape[0]):
        o2 = o_scr[hp]
        res = o2[(nh - 1) * blk:]
        for h in range(nh - 1):
            res = jnp.where(in_head[h], o2[h * blk:(h + 1) * blk], res)
        parts.append(res)
    return jnp.concatenate(parts, axis=1)


def _tail_stages(x_ref, mix_in, p_ref, apn_ref, fpre_ref, fpost_ref, wout_ref, wg_ref, wu_ref,
                 wd_ref, pp_ref, pg_ref, act_scr, o_ref):
    d = wout_ref.shape[1]
    sw = d // TAIL_SLABS
    slabs = [slice(j * sw, (j + 1) * sw) for j in range(TAIL_SLABS)]
    parts = []
    for sl in slabs:
        parts.append(jnp.dot(mix_in, wout_ref[:, sl], preferred_element_type=F32))
        yield
    h1 = x_ref[...] + _rms(jnp.concatenate(parts, axis=1)) * apn_ref[...]
    u = (_rms(h1) * fpre_ref[...]).astype(BF16)
    for j in range(act_scr.shape[1] // FF_SLAB):
        fs = slice(j * FF_SLAB, (j + 1) * FF_SLAB)
        g = jnp.dot(u, wg_ref[:, fs], preferred_element_type=F32)
        up = jnp.dot(u, wu_ref[:, fs], preferred_element_type=F32)
        act_scr[:, fs] = (g * jax.nn.sigmoid(g) * up).astype(BF16)
        yield
    act = act_scr[...]
    parts = []
    for sl in slabs:
        parts.append(jnp.dot(act, wd_ref[:, sl], preferred_element_type=F32))
        yield
    h2 = h1 + _rms(jnp.concatenate(parts, axis=1)) * fpost_ref[...]
    emb = jnp.dot(p_ref[...].astype(BF16), pp_ref[...], preferred_element_type=F32)
    h2b = h2.astype(BF16)
    for sl in slabs:
        gate = jnp.dot(h2b, pg_ref[:, sl], preferred_element_type=F32)
        o_ref[:, sl] = h2[:, sl] + emb[:, sl] * jax.nn.sigmoid(gate)
        yield


def _round_robin(gens):
    live = list(gens)
    while live:
        for gen in list(live):
            try:
                next(gen)
                yield
            except StopIteration:
                live.remove(gen)


def _interleave(streams):
    state = [[gen, 0, n] for gen, n in streams]
    while state:
        cur = min(state, key=lambda t: t[1] / t[2])
        try:
            next(cur[0])
            cur[1] += 1
        except StopIteration:
            state.remove(cur)


def _fused_body(hq_ref, hf_ref, hi_ref, hg_ref, sq_ref, k_ref, v_ref, x_ref, p_ref,
                gam_ref, hgn_ref, tri_ref, esel_ref, m_ref, sbn_ref,
                apn_ref, fpre_ref, fpost_ref, wout_ref, wg_ref, wu_ref, wd_ref, pp_ref, pg_ref,
                o_ref,
                mix_scr, st_ref, b_scr, lk_scr, k_scr, d_scr, so_scr, sc_scr, act_scr,
                *, layer, dh, blocks_per_seq):
    c = HG_CHUNK
    w = hq_ref.shape[1]
    nsub = hq_ref.shape[0] // c
    g = pl.program_id(0)
    last = pl.num_programs(0) - 1
    i = jnp.minimum(g, last - 1) % blocks_per_seq

    @pl.when(g % blocks_per_seq == 0)
    def _():
        st_ref[...] = jnp.zeros_like(st_ref)

    @pl.when(g == 0)
    def _():
        mix_scr[...] = jnp.zeros_like(mix_scr)

    slot = g % 2

    gam = gam_ref[...]
    e = jnp.exp(gam - jnp.max(gam, axis=0, keepdims=True))
    lb = jnp.sum(e[:layer + 1], axis=0, keepdims=True) / jnp.sum(e, axis=0, keepdims=True)

    def hgrn_emit(r):
        def emit(o):
            rows = pl.ds(r * c, c)
            gt = hg_ref[rows, :].astype(F32)
            mix_scr[slot, rows, 0:w] = (
                _rms(o) * hgn_ref[...] * (gt * jax.nn.sigmoid(gt))).astype(BF16)
        return emit

    hgrn = _round_robin([
        _hgrn_stages(hq_ref.at[pl.ds(r * c, c)], hf_ref.at[pl.ds(r * c, c)],
                     hi_ref.at[pl.ds(r * c, c)], lb, tri_ref, esel_ref, st_ref, b_scr.at[r],
                     lk_scr.at[r], k_scr.at[r], d_scr.at[r], hgrn_emit(r))
        for r in range(nsub)])

    strict, in_head = _sb_masks(dh)
    mcat = m_ref[...]
    q_refs = [sq_ref.at[pl.ds(r * c, c)] for r in range(nsub)]
    qis = [i * nsub + r for r in range(nsub)]
    o_scrs = [so_scr.at[r] for r in range(nsub)]
    c_scrs = [sc_scr.at[r] for r in range(nsub)]
    cmins = []
    sb = _sb_stages(q_refs, qis, k_ref, v_ref, mcat, strict, in_head, o_scrs, c_scrs, cmins.append)

    tail = _tail_stages(x_ref, mix_scr[1 - slot], p_ref, apn_ref, fpre_ref, fpost_ref, wout_ref,
                        wg_ref, wu_ref, wd_ref, pp_ref, pg_ref, act_scr, o_ref)
    n_tail = 3 * TAIL_SLABS + act_scr.shape[1] // FF_SLAB
    n_sb = 3 * (nsub * (w // LANES) // 2) + nsub * (w // LANES)
    _interleave([(tail, n_tail), (hgrn, nsub * (2 + 2 * HG_HEADS)), (sb, n_sb)])

    _sb_rest(q_refs, qis, k_ref, v_ref, mcat, in_head, o_scrs, c_scrs, cmins[0])
    for r in range(nsub):
        o = _sb_finish(o_scrs[r], in_head)
        mix_scr[slot, pl.ds(r * c, c), w:2 * w] = (_rms(o) * sbn_ref[...]).astype(BF16)


def _fused(hq, hf, hi, hg, sq, sk, sv, x2, p2, gamma, hgn, sbn, apn, fpre, fpost,
           wout, wg, wu, wd, pp, pg, *, layer, dh, tm):
    bsz, t, w = hq.shape
    n, d = x2.shape
    c = HG_CHUNK
    nsub = tm // c
    bps = t // tm
    nblk = n // tm
    tri = jnp.tril(jnp.ones((c, c), F32)).astype(BF16)
    sel = (jnp.arange(HG_SUB * LANES)[:, None] // LANES) == (jnp.arange(LANES)[None, :] % HG_SUB)
    esel = sel.astype(BF16)
    blk = SB_BLOCK
    later = jnp.arange(blk)[:, None] > jnp.arange(blk)[None, :]
    mcat = jnp.concatenate([later, jnp.ones((blk, blk), bool)], axis=1).astype(BF16)
    nh = LANES // dh

    def mixer_idx(g):
        gm = jnp.minimum(g, nblk - 1)
        return (gm // bps, gm % bps, 0)

    mspec = pl.BlockSpec((None, tm, w), mixer_idx)
    kvspec = pl.BlockSpec((None, t, w), lambda g: (jnp.minimum(g, nblk - 1) // bps, 0, 0),
                          pipeline_mode=pl.Buffered(1))
    row = lambda a: pl.BlockSpec((tm, a.shape[1]), lambda g: (jnp.maximum(g - 1, 0), 0))
    consts = (gamma, hgn, tri, esel, mcat, sbn, apn, fpre, fpost, wout, wg, wu, wd, pp, pg)
    return pl.pallas_call(
        functools.partial(_fused_body, layer=layer, dh=dh, blocks_per_seq=bps),
        grid=(nblk + 1,),
        in_specs=[mspec] * 5 + [kvspec, kvspec, row(x2), row(p2)]
                 + [_const_spec(a.shape) for a in consts],
        out_specs=row(x2),
        out_shape=jax.ShapeDtypeStruct((n, d), F32),
        scratch_shapes=[
            pltpu.VMEM((2, tm, 2 * w), BF16),
            pltpu.VMEM((HG_HEADS, LANES, LANES), F32),
            pltpu.VMEM((nsub, HG_HEADS, c, LANES), F32),
            pltpu.VMEM((nsub, HG_HEADS, c, LANES), F32),
            pltpu.VMEM((nsub, HG_HEADS, c, LANES), F32),
            pltpu.VMEM((nsub, HG_HEADS, c, HG_SUB * LANES), BF16),
            pltpu.VMEM((nsub, w // LANES, nh * blk, LANES), F32),
            pltpu.VMEM((nsub, w // LANES, nh * blk, blk), F32),
            pltpu.VMEM((tm, wd.shape[0]), BF16),
        ],
        compiler_params=pltpu.CompilerParams(
            dimension_semantics=("arbitrary",), vmem_limit_bytes=VMEM_LIMIT),
        name="mix_tail",
    )(hq, hf, hi, hg, sq, sk, sv, x2, p2, *consts)


def kernel(x, p, attn_pre_norm, w_in, hg_lower_gamma, hg_out_norm, sb_out_norm, w_out, attn_post_norm, ffn_pre_norm, w_gate_up, w_down, ffn_post_norm, ple_proj, ple_gate):
    bsz, t, d = x.shape
    depth = w_in.shape[0]
    n = bsz * t
    width = hg_out_norm.shape[1]
    assert sb_out_norm.shape[1] == width and w_in.shape[2] == 7 * width
    assert width == HG_HEADS * LANES and HG_CHUNK == SB_BLOCK
    dh = width // SB_HEADS
    d_ff = w_down.shape[1]
    assert d_ff % FF_SLAB == 0 and d % (TAIL_SLABS * LANES) == 0
    tm = 512
    tm_fused = 256
    assert n % tm == 0 and t % tm_fused == 0 and tm_fused % HG_CHUNK == 0

    h = x.reshape(n, d)
    for i in range(depth):
        hq, hf, hi, hg, sq, sk, sv = _inproj(
            h, attn_pre_norm[i][None], w_in[i].astype(BF16),
            width=width, sb_scale=dh ** -0.5, tm=tm)
        to3 = lambda a: a.reshape(bsz, t, width)
        wgu = w_gate_up[i].astype(BF16)
        h = _fused(
            to3(hq), to3(hf), to3(hi), to3(hg), to3(sq), to3(sk), to3(sv), h, p[i].reshape(n, -1),
            hg_lower_gamma, hg_out_norm[i][None], sb_out_norm[i][None],
            attn_post_norm[i][None], ffn_pre_norm[i][None], ffn_post_norm[i][None],
            w_out[i].astype(BF16), wgu[:, :d_ff], wgu[:, d_ff:], w_down[i].astype(BF16),
            ple_proj[i].astype(BF16), ple_gate[i].astype(BF16), layer=i, dh=dh, tm=tm_fused)
    return h.reshape(bsz, t, d)
```

```python
import functools

import jax
import jax.numpy as jnp
from jax import lax
from jax.experimental import pallas as pl
from jax.experimental.pallas import tpu as pltpu

F32 = jnp.float32
BF16 = jnp.bfloat16
EPS = 1e-6
LOG2E = 1.4426950408889634

HG_HEADS = 4
SB_HEADS = 8
LANES = 128
HG_CHUNK = 128
HG_SUB = 8
SB_Q = 64
SB_WIN = 256
SB_KB = 128
SB_DONE = 115.0
FF_SLAB = 256
TAIL_SLABS = 4
VMEM_LIMIT = 56 * 1024 * 1024

_NT = (((1,), (1,)), ((), ()))
_TN = (((0,), (0,)), ((), ()))


def _rms(xf):
    return xf * lax.rsqrt(jnp.mean(xf * xf, axis=-1, keepdims=True) + EPS)


def _softplus(z):
    return jnp.maximum(z, 0.0) + jnp.log(1.0 + jnp.exp2(jnp.abs(z) * -LOG2E))


def _logaddexp(a, c):
    return jnp.maximum(a, c) + jnp.log(1.0 + jnp.exp(-jnp.abs(a - c)))


def _const_spec(shape):
    nd = len(shape)
    return pl.BlockSpec(shape, lambda *_: (0,) * nd, pipeline_mode=pl.Buffered(1))


def _inproj_body(x_ref, nw_ref, w_ref, hq_ref, hf_ref, hi_ref, hg_ref, sq_ref, sk_ref, sv_ref,
                 *, width, sb_scale):
    u = (_rms(x_ref[...]) * nw_ref[...]).astype(BF16)
    outs = (hq_ref, hf_ref, hi_ref, hg_ref, sq_ref, sk_ref, sv_ref)
    for j, o_ref in enumerate(outs):
        r = jnp.dot(u, w_ref[:, j * width:(j + 1) * width], preferred_element_type=F32)
        if o_ref is sq_ref:
            r = r * sb_scale
        o_ref[...] = r.astype(o_ref.dtype)


def _inproj(x2, nw, w_in, *, width, sb_scale, tm):
    n, d = x2.shape
    out_dtypes = (BF16, F32, BF16, BF16, BF16, BF16, BF16)
    return pl.pallas_call(
        functools.partial(_inproj_body, width=width, sb_scale=sb_scale),
        grid=(n // tm,),
        in_specs=[
            pl.BlockSpec((tm, d), lambda i: (i, 0)),
            _const_spec((1, d)),
            _const_spec(w_in.shape),
        ],
        out_specs=[pl.BlockSpec((tm, width), lambda i: (i, 0)) for _ in out_dtypes],
        out_shape=[jax.ShapeDtypeStruct((n, width), dt) for dt in out_dtypes],
        compiler_params=pltpu.CompilerParams(
            dimension_semantics=("parallel",), vmem_limit_bytes=VMEM_LIMIT),
        name="inproj",
    )(x2, nw, w_in)


def _hgrn_stages(hq_ref, hf_ref, hi_ref, lb, tri_ref, esel_ref, st_ref, b_scr, lk_scr, k_scr, d_scr,
                 emit):
    c = HG_CHUNK
    sub = HG_SUB

    qz = hq_ref[...].astype(F32)
    q = qz * jax.nn.sigmoid(qz)
    z = hf_ref[...]
    log_sig = jnp.minimum(z, 0.0) - jnp.log(1.0 + jnp.exp(-jnp.abs(z)))
    log_1mlb = jnp.log1p(-lb)
    logf = _logaddexp(jnp.log(lb), log_1mlb + log_sig) * LOG2E
    logk = (log_1mlb + log_sig - z) * LOG2E
    lhi = logf.astype(BF16)
    llo = (logf - lhi.astype(F32)).astype(BF16)
    tri = tri_ref[...]
    b = (jnp.dot(tri, lhi, preferred_element_type=F32)
         + jnp.dot(tri, llo, preferred_element_type=F32))
    lk = logk - b
    for h in range(HG_HEADS):
        cs = slice(h * LANES, (h + 1) * LANES)
        b_scr[h] = b[:, cs]
        lk_scr[h] = lk[:, cs]
        k_scr[h] = logk[:, cs]
    yield

    two = 2 * sub
    for h in range(HG_HEADS):
        cs = slice(h * LANES, (h + 1) * LANES)
        for r0 in range(0, c, two):
            bi = b[r0:r0 + two, cs]
            qi = q[r0:r0 + two, cs]
            for s in range(sub):
                def key_row(ref):
                    return jnp.concatenate([ref[h, pl.ds(r0 + s, sub, stride=0), :],
                                            ref[h, pl.ds(r0 + sub + s, sub, stride=0), :]], axis=0)
                d = qi * jnp.exp2(jnp.minimum(bi + key_row(lk_scr), key_row(k_scr)))
                d_scr[h, r0:r0 + two, s * LANES:(s + 1) * LANES] = d.astype(BF16)
        yield

    row = lax.broadcasted_iota(jnp.int32, (c, LANES), 0)
    col = lax.broadcasted_iota(jnp.int32, (c, LANES), 1)
    keep_diag = ((col // sub) == (row // sub)) & ((col % sub) <= (row % sub))
    half = c // 2
    rowq = lax.broadcasted_iota(jnp.int32, (half, LANES), 0)
    colq = lax.broadcasted_iota(jnp.int32, (half, LANES), 1)
    levels = []
    span = half
    while span >= sub:
        levels.append((span, ((colq // (2 * span)) == (rowq // span)) & ((colq % (2 * span)) < span)))
        span //= 2

    outs = []
    for h in range(HG_HEADS):
        cs = slice(h * LANES, (h + 1) * LANES)
        b_h, q_h, lk_h = b[:, cs], q[:, cs], lk[:, cs]
        v = hi_ref[:, cs]
        b_last = b_scr[h, c - 1:c, :]

        st = st_ref[h]
        o_h = lax.dot_general((q_h * jnp.exp2(b_h)).astype(BF16), st.astype(BF16), _NT,
                              preferred_element_type=F32)
        kh = jnp.exp2(lk_h + b_last).astype(BF16)
        st_ref[h] = st * jnp.exp2(b_last) + lax.dot_general(
            v, kh, _TN, preferred_element_type=F32)

        sdiag = jnp.where(keep_diag, jnp.dot(d_scr[h], esel_ref[...], preferred_element_type=F32), 0.0)
        pieces = [sdiag[i:i + 8] for i in range(0, c, 8)]

        for span, keep in levels:
            qparts, kparts = [], []
            for lo in range(0, c, 2 * span):
                mid, hi = lo + span, lo + 2 * span
                r = b_scr[h, mid - 1:mid, :]
                qparts.append(q_h[mid:hi] * jnp.exp2(b_h[mid:hi] - r))
                kparts.append(jnp.exp2(lk_h[lo:mid] + r))
                kparts.append(jnp.zeros((span, LANES), F32))
            sc = lax.dot_general(jnp.concatenate(qparts, axis=0).astype(BF16),
                                 jnp.concatenate(kparts, axis=0).astype(BF16), _NT,
                                 preferred_element_type=F32)
            sc = jnp.where(keep, sc, 0.0)
            for g, lo in enumerate(range(0, c, 2 * span)):
                for m in range(span // 8):
                    idx = (lo + span) // 8 + m
                    pieces[idx] = pieces[idx] + sc[g * span + 8 * m:g * span + 8 * m + 8]
        scores = jnp.concatenate(pieces, axis=0).astype(BF16)
        outs.append(o_h + jnp.dot(scores, v, preferred_element_type=F32))
        yield
    emit(jnp.concatenate(outs, axis=1))


def _sb_consts(dh):
    nh = LANES // dh
    lane = lax.broadcasted_iota(jnp.int32, (SB_Q, LANES), 1)
    in_head = [(lane >= h * dh) & (lane < (h + 1) * dh) for h in range(nh)]
    row = lax.broadcasted_iota(jnp.int32, (nh * SB_Q, SB_WIN), 0) & (SB_Q - 1)
    col = lax.broadcasted_iota(jnp.int32, (nh * SB_Q, SB_WIN), 1)
    return in_head, col - row


def _head_rows(q, in_head):
    return jnp.concatenate([jnp.where(m, q, jnp.zeros_like(q)) for m in in_head], axis=0)


def _sb_stages(q_refs, q0s, k_ref, v_ref, u_ref, in_head, col_minus_row, o_scrs, c_scrs, emit):
    npairs = q_refs[0].shape[1] // LANES
    chains = [(r, hp) for r in range(len(q_refs)) for hp in range(npairs)]
    s0s = [pl.multiple_of(jnp.maximum(q0 - (SB_WIN - SB_Q), 0), SB_Q) for q0 in q0s]
    valids = [col_minus_row < (q0 - s0) for q0, s0 in zip(q0s, s0s)]
    zs = []
    for n, (r, hp) in enumerate(chains):
        cs = slice(hp * LANES, (hp + 1) * LANES)
        zs.append(lax.dot_general(_head_rows(q_refs[r][:, cs], in_head),
                                  k_ref[pl.ds(s0s[r], SB_WIN), cs], _NT,
                                  preferred_element_type=F32))
        if n % 4 == 3:
            yield
    sps = []
    for n, z in enumerate(zs):
        sps.append(_softplus(z))
        if n % 4 == 3:
            yield
    rts = []
    for n, (r, hp) in enumerate(chains):
        spm = jnp.where(valids[r], sps[n], 0.0).astype(BF16)
        rts.append(jnp.dot(spm, u_ref[...], preferred_element_type=F32))
        if n % 4 == 3:
            yield
    cmin = None
    for n, (r, hp) in enumerate(chains):
        cs = slice(hp * LANES, (hp + 1) * LANES)
        rt = rts[n]
        a = jnp.where(valids[r], jnp.exp(zs[n] - sps[n] - rt[:, :SB_WIN]), 0.0).astype(BF16)
        o_scrs[r][hp] = jnp.dot(a, v_ref[pl.ds(s0s[r], SB_WIN), cs], preferred_element_type=F32)
        c = rt[:, SB_WIN:]
        c_scrs[r][hp] = c
        cmin = c if cmin is None else jnp.minimum(cmin, c)
        if n % 2:
            yield
    emit(cmin)


def _sb_rest(q_refs, q0s, k_ref, v_ref, m_ref, in_head, o_scrs, c_scrs, cmin):
    npairs = q_refs[0].shape[1] // LANES
    nh = len(in_head)
    s0s = [jnp.maximum(q0 - (SB_WIN - SB_Q), 0) for q0 in q0s]
    col = lax.broadcasted_iota(jnp.int32, (nh * SB_Q, SB_KB), 1)

    def more(carry):
        d, cm = carry
        return (s0s[-1] - d * SB_KB > 0) & (cm < SB_DONE)

    def one_block(carry):
        d, _ = carry
        cmin = None
        for r in range(len(q_refs)):
            end = s0s[r] - d * SB_KB
            start = pl.multiple_of(jnp.maximum(end - SB_KB, 0), SB_Q)
            valid = col < (end - start)
            for hp in range(npairs):
                cs = slice(hp * LANES, (hp + 1) * LANES)
                qs = _head_rows(q_refs[r][:, cs], in_head)
                z = lax.dot_general(qs, k_ref[pl.ds(start, SB_KB), cs], _NT, preferred_element_type=F32)
                sp = _softplus(z)
                rt = jnp.dot(jnp.where(valid, sp, 0.0).astype(BF16), m_ref[...],
                             preferred_element_type=F32)
                c = c_scrs[r][hp]
                a = jnp.where(valid, jnp.exp(z - sp - rt[:, :SB_KB] - c), 0.0)
                o_scrs[r][hp] += jnp.dot(a.astype(BF16), v_ref[pl.ds(start, SB_KB), cs],
                                         preferred_element_type=F32)
                c = c + rt[:, SB_KB:]
                c_scrs[r][hp] = c
                cmin = c if cmin is None else jnp.minimum(cmin, c)
        return d + 1, jnp.min(cmin)

    lax.while_loop(more, one_block, (jnp.int32(0), jnp.min(cmin)))


def _sb_finish(o_scr, in_head):
    nh = len(in_head)
    parts = []
    for hp in range(o_scr.shape[0]):
        o2 = o_scr[hp]
        res = o2[(nh - 1) * SB_Q:]
        for h in range(nh - 1):
            res = jnp.where(in_head[h], o2[h * SB_Q:(h + 1) * SB_Q], res)
        parts.append(res)
    return jnp.concatenate(parts, axis=1)


def _tail_stages(x_ref, mix_in, p_ref, apn_ref, fpre_ref, fpost_ref, wout_ref, wg_ref, wu_ref,
                 wd_ref, pp_ref, pg_ref, act_scr, o_ref):
    d = wout_ref.shape[1]
    sw = d // TAIL_SLABS
    slabs = [slice(j * sw, (j + 1) * sw) for j in range(TAIL_SLABS)]
    parts = []
    for sl in slabs:
        parts.append(jnp.dot(mix_in, wout_ref[:, sl], preferred_element_type=F32))
        yield
    h1 = x_ref[...] + _rms(jnp.concatenate(parts, axis=1)) * apn_ref[...]
    u = (_rms(h1) * fpre_ref[...]).astype(BF16)
    for j in range(act_scr.shape[1] // FF_SLAB):
        fs = slice(j * FF_SLAB, (j + 1) * FF_SLAB)
        g = jnp.dot(u, wg_ref[:, fs], preferred_element_type=F32)
        up = jnp.dot(u, wu_ref[:, fs], preferred_element_type=F32)
        act_scr[:, fs] = (g * jax.nn.sigmoid(g) * up).astype(BF16)
        yield
    act = act_scr[...]
    parts = []
    for sl in slabs:
        parts.append(jnp.dot(act, wd_ref[:, sl], preferred_element_type=F32))
        yield
    h2 = h1 + _rms(jnp.concatenate(parts, axis=1)) * fpost_ref[...]
    emb = jnp.dot(p_ref[...].astype(BF16), pp_ref[...], preferred_element_type=F32)
    h2b = h2.astype(BF16)
    for sl in slabs:
        gate = jnp.dot(h2b, pg_ref[:, sl], preferred_element_type=F32)
        o_ref[:, sl] = h2[:, sl] + emb[:, sl] * jax.nn.sigmoid(gate)
        yield


def _round_robin(gens):
    live = list(gens)
    while live:
        for gen in list(live):
            try:
                next(gen)
                yield
            except StopIteration:
                live.remove(gen)


def _interleave(streams):
    state = [[gen, 0, n] for gen, n in streams]
    while state:
        cur = min(state, key=lambda t: t[1] / t[2])
        try:
            next(cur[0])
            cur[1] += 1
        except StopIteration:
            state.remove(cur)


def _fused_body(hq_ref, hf_ref, hi_ref, hg_ref, sq_ref, k_ref, v_ref, x_ref, p_ref,
                gam_ref, hgn_ref, tri_ref, esel_ref, u_ref, m_ref, sbn_ref,
                apn_ref, fpre_ref, fpost_ref, wout_ref, wg_ref, wu_ref, wd_ref, pp_ref, pg_ref,
                o_ref,
                mix_scr, st_ref, b_scr, lk_scr, k_scr, d_scr, so_scr, sc_scr, act_scr,
                *, layer, dh, blocks_per_seq):
    c = HG_CHUNK
    w = hq_ref.shape[1]
    nsub = hq_ref.shape[0] // c
    g = pl.program_id(0)
    last = pl.num_programs(0) - 1
    i = jnp.minimum(g, last - 1) % blocks_per_seq

    @pl.when(g % blocks_per_seq == 0)
    def _():
        st_ref[...] = jnp.zeros_like(st_ref)

    @pl.when(g == 0)
    def _():
        mix_scr[...] = jnp.zeros_like(mix_scr)

    slot = g % 2

    gam = gam_ref[...]
    e = jnp.exp(gam - jnp.max(gam, axis=0, keepdims=True))
    lb = jnp.sum(e[:layer + 1], axis=0, keepdims=True) / jnp.sum(e, axis=0, keepdims=True)

    def hgrn_emit(r):
        def emit(o):
            rows = pl.ds(r * c, c)
            gt = hg_ref[rows, :].astype(F32)
            mix_scr[slot, rows, 0:w] = (
                _rms(o) * hgn_ref[...] * (gt * jax.nn.sigmoid(gt))).astype(BF16)
        return emit

    hgrn = _round_robin([
        _hgrn_stages(hq_ref.at[pl.ds(r * c, c)], hf_ref.at[pl.ds(r * c, c)],
                     hi_ref.at[pl.ds(r * c, c)], lb, tri_ref, esel_ref, st_ref, b_scr.at[r],
                     lk_scr.at[r], k_scr.at[r], d_scr.at[r], hgrn_emit(r))
        for r in range(nsub)])

    in_head, col_minus_row = _sb_consts(dh)
    nq = hq_ref.shape[0] // SB_Q
    q_refs = [sq_ref.at[pl.ds(r * SB_Q, SB_Q)] for r in range(nq)]
    q0s = [(i * nq + r) * SB_Q for r in range(nq)]
    o_scrs = [so_scr.at[r] for r in range(nq)]
    c_scrs = [sc_scr.at[r] for r in range(nq)]
    cmins = []
    sb = _sb_stages(q_refs, q0s, k_ref, v_ref, u_ref, in_head, col_minus_row, o_scrs, c_scrs,
                    cmins.append)

    tail = _tail_stages(x_ref, mix_scr[1 - slot], p_ref, apn_ref, fpre_ref, fpost_ref, wout_ref,
                        wg_ref, wu_ref, wd_ref, pp_ref, pg_ref, act_scr, o_ref)
    n_tail = 3 * TAIL_SLABS + act_scr.shape[1] // FF_SLAB
    n_sb = 3 * (nq * (w // LANES) // 4) + nq * (w // LANES) // 2
    _interleave([(tail, n_tail), (hgrn, nsub * (2 + 2 * HG_HEADS)), (sb, n_sb)])

    _sb_rest(q_refs, q0s, k_ref, v_ref, m_ref, in_head, o_scrs, c_scrs, cmins[0])
    for r in range(nq):
        o = _sb_finish(o_scrs[r], in_head)
        mix_scr[slot, pl.ds(r * SB_Q, SB_Q), w:2 * w] = (_rms(o) * sbn_ref[...]).astype(BF16)


def _fused(hq, hf, hi, hg, sq, sk, sv, x2, p2, gamma, hgn, sbn, apn, fpre, fpost,
           wout, wg, wu, wd, pp, pg, *, layer, dh, tm):
    bsz, t, w = hq.shape
    n, d = x2.shape
    c = HG_CHUNK
    nsub = tm // c
    bps = t // tm
    nblk = n // tm
    tri = jnp.tril(jnp.ones((c, c), F32)).astype(BF16)
    sel = (jnp.arange(HG_SUB * LANES)[:, None] // LANES) == (jnp.arange(LANES)[None, :] % HG_SUB)
    esel = sel.astype(BF16)
    def later_and_ones(nk):
        later = jnp.arange(nk)[:, None] > jnp.arange(nk)[None, :]
        return jnp.concatenate([later, jnp.ones((nk, LANES), bool)], axis=1).astype(BF16)

    ucat, mcat = later_and_ones(SB_WIN), later_and_ones(SB_KB)
    nh = LANES // dh
    nq = tm // SB_Q

    def mixer_idx(g):
        gm = jnp.minimum(g, nblk - 1)
        return (gm // bps, gm % bps, 0)

    mspec = pl.BlockSpec((None, tm, w), mixer_idx)
    kvspec = pl.BlockSpec((None, t, w), lambda g: (jnp.minimum(g, nblk - 1) // bps, 0, 0),
                          pipeline_mode=pl.Buffered(1))
    row = lambda a: pl.BlockSpec((tm, a.shape[1]), lambda g: (jnp.maximum(g - 1, 0), 0))
    consts = (gamma, hgn, tri, esel, ucat, mcat, sbn, apn, fpre, fpost, wout, wg, wu, wd, pp, pg)
    return pl.pallas_call(
        functools.partial(_fused_body, layer=layer, dh=dh, blocks_per_seq=bps),
        grid=(nblk + 1,),
        in_specs=[mspec] * 5 + [kvspec, kvspec, row(x2), row(p2)]
                 + [_const_spec(a.shape) for a in consts],
        out_specs=row(x2),
        out_shape=jax.ShapeDtypeStruct((n, d), F32),
        scratch_shapes=[
            pltpu.VMEM((2, tm, 2 * w), BF16),
            pltpu.VMEM((HG_HEADS, LANES, LANES), F32),
            pltpu.VMEM((nsub, HG_HEADS, c, LANES), F32),
            pltpu.VMEM((nsub, HG_HEADS, c, LANES), F32),
            pltpu.VMEM((nsub, HG_HEADS, c, LANES), F32),
            pltpu.VMEM((nsub, HG_HEADS, c, HG_SUB * LANES), BF16),
            pltpu.VMEM((nq, w // LANES, nh * SB_Q, LANES), F32),
            pltpu.VMEM((nq, w // LANES, nh * SB_Q, LANES), F32),
            pltpu.VMEM((tm, wd.shape[0]), BF16),
        ],
        compiler_params=pltpu.CompilerParams(
            dimension_semantics=("arbitrary",), vmem_limit_bytes=VMEM_LIMIT),
        name="mix_tail",
    )(hq, hf, hi, hg, sq, sk, sv, x2, p2, *consts)


def kernel(x, p, attn_pre_norm, w_in, hg_lower_gamma, hg_out_norm, sb_out_norm, w_out, attn_post_norm, ffn_pre_norm, w_gate_up, w_down, ffn_post_norm, ple_proj, ple_gate):
    bsz, t, d = x.shape
    depth = w_in.shape[0]
    n = bsz * t
    width = hg_out_norm.shape[1]
    assert sb_out_norm.shape[1] == width and w_in.shape[2] == 7 * width
    assert width == HG_HEADS * LANES and t >= SB_WIN
    dh = width // SB_HEADS
    d_ff = w_down.shape[1]
    assert d_ff % FF_SLAB == 0 and d % (TAIL_SLABS * LANES) == 0
    tm = 512
    tm_fused = 256
    assert n % tm == 0 and t % tm_fused == 0 and tm_fused % HG_CHUNK == 0 and tm_fused % SB_Q == 0

    h = x.reshape(n, d)
    for i in range(depth):
        hq, hf, hi, hg, sq, sk, sv = _inproj(
            h, attn_pre_norm[i][None], w_in[i].astype(BF16),
            width=width, sb_scale=dh ** -0.5, tm=tm)
        to3 = lambda a: a.reshape(bsz, t, width)
        wgu = w_gate_up[i].astype(BF16)
        h = _fused(
            to3(hq), to3(hf), to3(hi), to3(hg), to3(sq), to3(sk), to3(sv), h, p[i].reshape(n, -1),
            hg_lower_gamma, hg_out_norm[i][None], sb_out_norm[i][None],
            attn_post_norm[i][None], ffn_pre_norm[i][None], ffn_post_norm[i][None],
            w_out[i].astype(BF16), wgu[:, :d_ff], wgu[:, d_ff:], w_down[i].astype(BF16),
            ple_proj[i].astype(BF16), ple_gate[i].astype(BF16), layer=i, dh=dh, tm=tm_fused)
    return h.reshape(bsz, t, d)
```

```python
import functools

import jax
import jax.numpy as jnp
from jax import lax
from jax.experimental import pallas as pl
from jax.experimental.pallas import tpu as pltpu

F32 = jnp.float32
BF16 = jnp.bfloat16
EPS = 1e-6
LOG2E = 1.4426950408889634

HG_HEADS = 4
SB_HEADS = 8
LANES = 128
HG_CHUNK = 128
HG_SUB = 8
SB_Q = 64
SB_WIN = 256
SB_KB = 128
SB_DONE = 115.0
FF_SLAB = 256
TAIL_SLABS = 4
VMEM_LIMIT = 56 * 1024 * 1024

_NT = (((1,), (1,)), ((), ()))
_TN = (((0,), (0,)), ((), ()))


def _rms(xf):
    return xf * lax.rsqrt(jnp.mean(xf * xf, axis=-1, keepdims=True) + EPS)


def _softplus(z):
    return jnp.maximum(z, 0.0) + jnp.log(1.0 + jnp.exp2(jnp.abs(z) * -LOG2E))


def _logaddexp(a, c):
    return jnp.maximum(a, c) + jnp.log(1.0 + jnp.exp(-jnp.abs(a - c)))


def _const_spec(shape):
    nd = len(shape)
    return pl.BlockSpec(shape, lambda *_: (0,) * nd, pipeline_mode=pl.Buffered(1))


def _inproj_body(x_ref, nw_ref, w_ref, hq_ref, hf_ref, hi_ref, hg_ref, sq_ref, sk_ref, sv_ref,
                 *, width, sb_scale):
    u = (_rms(x_ref[...]) * nw_ref[...]).astype(BF16)
    outs = (hq_ref, hf_ref, hi_ref, hg_ref, sq_ref, sk_ref, sv_ref)
    for j, o_ref in enumerate(outs):
        r = jnp.dot(u, w_ref[:, j * width:(j + 1) * width], preferred_element_type=F32)
        if o_ref is sq_ref:
            r = r * sb_scale
        o_ref[...] = r.astype(o_ref.dtype)


def _inproj(x2, nw, w_in, *, width, sb_scale, tm):
    n, d = x2.shape
    out_dtypes = (BF16, F32, BF16, BF16, BF16, BF16, BF16)
    return pl.pallas_call(
        functools.partial(_inproj_body, width=width, sb_scale=sb_scale),
        grid=(n // tm,),
        in_specs=[
            pl.BlockSpec((tm, d), lambda i: (i, 0)),
            _const_spec((1, d)),
            _const_spec(w_in.shape),
        ],
        out_specs=[pl.BlockSpec((tm, width), lambda i: (i, 0)) for _ in out_dtypes],
        out_shape=[jax.ShapeDtypeStruct((n, width), dt) for dt in out_dtypes],
        compiler_params=pltpu.CompilerParams(
            dimension_semantics=("parallel",), vmem_limit_bytes=VMEM_LIMIT),
        name="inproj",
    )(x2, nw, w_in)


def _hgrn_stages(hq_ref, hf_ref, hi_ref, lb, tri_ref, esel_ref, st_ref, b_scr, lk_scr, k_scr, d_scr,
                 emit):
    c = HG_CHUNK
    sub = HG_SUB

    qz = hq_ref[...].astype(F32)
    q = qz * jax.nn.sigmoid(qz)
    z = hf_ref[...]
    log_sig = jnp.minimum(z, 0.0) - jnp.log(1.0 + jnp.exp2(jnp.abs(z) * -LOG2E))
    log_1mlb = jnp.log1p(-lb)
    logf = _logaddexp(jnp.log(lb), log_1mlb + log_sig) * LOG2E
    logk = (log_1mlb + log_sig - z) * LOG2E
    lhi = logf.astype(BF16)
    llo = (logf - lhi.astype(F32)).astype(BF16)
    tri = tri_ref[...]
    b = (jnp.dot(tri, lhi, preferred_element_type=F32)
         + jnp.dot(tri, llo, preferred_element_type=F32))
    lk = logk - b
    for h in range(HG_HEADS):
        cs = slice(h * LANES, (h + 1) * LANES)
        b_scr[h] = b[:, cs]
        lk_scr[h] = lk[:, cs]
        k_scr[h] = logk[:, cs]
    yield

    two = 2 * sub
    for h in range(HG_HEADS):
        cs = slice(h * LANES, (h + 1) * LANES)
        for r0 in range(0, c, two):
            bi = b[r0:r0 + two, cs]
            qi = q[r0:r0 + two, cs]
            for s in range(sub):
                def key_row(ref):
                    return jnp.concatenate([ref[h, pl.ds(r0 + s, sub, stride=0), :],
                                            ref[h, pl.ds(r0 + sub + s, sub, stride=0), :]], axis=0)
                d = qi * jnp.exp2(jnp.minimum(bi + key_row(lk_scr), key_row(k_scr)))
                d_scr[h, r0:r0 + two, s * LANES:(s + 1) * LANES] = d.astype(BF16)
            if (r0 // two) % 4 == 3:
                yield

    row = lax.broadcasted_iota(jnp.int32, (c, LANES), 0)
    col = lax.broadcasted_iota(jnp.int32, (c, LANES), 1)
    keep_diag = ((col // sub) == (row // sub)) & ((col % sub) <= (row % sub))
    half = c // 2
    rowq = lax.broadcasted_iota(jnp.int32, (half, LANES), 0)
    colq = lax.broadcasted_iota(jnp.int32, (half, LANES), 1)
    levels = []
    span = half
    while span >= sub:
        levels.append((span, ((colq // (2 * span)) == (rowq // span)) & ((colq % (2 * span)) < span)))
        span //= 2

    outs = []
    for h in range(HG_HEADS):
        cs = slice(h * LANES, (h + 1) * LANES)
        b_h, q_h, lk_h = b[:, cs], q[:, cs], lk[:, cs]
        v = hi_ref[:, cs]
        b_last = b_scr[h, c - 1:c, :]

        st = st_ref[h]
        o_h = lax.dot_general((q_h * jnp.exp2(b_h)).astype(BF16), st.astype(BF16), _NT,
                              preferred_element_type=F32)
        kh = jnp.exp2(lk_h + b_last).astype(BF16)
        st_ref[h] = st * jnp.exp2(b_last) + lax.dot_general(
            v, kh, _TN, preferred_element_type=F32)

        sdiag = jnp.where(keep_diag, jnp.dot(d_scr[h], esel_ref[...], preferred_element_type=F32), 0.0)
        pieces = [sdiag[i:i + 8] for i in range(0, c, 8)]

        for span, keep in levels:
            qparts, kparts = [], []
            for lo in range(0, c, 2 * span):
                mid, hi = lo + span, lo + 2 * span
                r = b_scr[h, mid - 1:mid, :]
                qparts.append(q_h[mid:hi] * jnp.exp2(b_h[mid:hi] - r))
                kparts.append(jnp.exp2(lk_h[lo:mid] + r))
                kparts.append(jnp.zeros((span, LANES), F32))
            sc = lax.dot_general(jnp.concatenate(qparts, axis=0).astype(BF16),
                                 jnp.concatenate(kparts, axis=0).astype(BF16), _NT,
                                 preferred_element_type=F32)
            sc = jnp.where(keep, sc, 0.0)
            for g, lo in enumerate(range(0, c, 2 * span)):
                for m in range(span // 8):
                    idx = (lo + span) // 8 + m
                    pieces[idx] = pieces[idx] + sc[g * span + 8 * m:g * span + 8 * m + 8]
        scores = jnp.concatenate(pieces, axis=0).astype(BF16)
        outs.append(o_h + jnp.dot(scores, v, preferred_element_type=F32))
        yield
    emit(jnp.concatenate(outs, axis=1))


def _sb_consts(dh):
    nh = LANES // dh
    lane = lax.broadcasted_iota(jnp.int32, (SB_Q, LANES), 1)
    in_head = [(lane >= h * dh) & (lane < (h + 1) * dh) for h in range(nh)]
    row = lax.broadcasted_iota(jnp.int32, (nh * SB_Q, SB_WIN), 0) & (SB_Q - 1)
    col = lax.broadcasted_iota(jnp.int32, (nh * SB_Q, SB_WIN), 1)
    return in_head, col - row


def _head_rows(q, in_head):
    return jnp.concatenate([jnp.where(m, q, jnp.zeros_like(q)) for m in in_head], axis=0)


def _sb_stages(q_refs, q0s, k_ref, v_ref, u_ref, in_head, col_minus_row, o_scrs, c_scrs, emit):
    npairs = q_refs[0].shape[1] // LANES
    chains = [(r, hp) for r in range(len(q_refs)) for hp in range(npairs)]
    s0s = [pl.multiple_of(jnp.maximum(q0 - (SB_WIN - SB_Q), 0), SB_Q) for q0 in q0s]
    valids = [col_minus_row < (q0 - s0) for q0, s0 in zip(q0s, s0s)]
    zs = []
    for n, (r, hp) in enumerate(chains):
        cs = slice(hp * LANES, (hp + 1) * LANES)
        zs.append(lax.dot_general(_head_rows(q_refs[r][:, cs], in_head),
                                  k_ref[pl.ds(s0s[r], SB_WIN), cs], _NT,
                                  preferred_element_type=F32))
        if n % 4 == 3:
            yield
    sps = []
    for n, z in enumerate(zs):
        sps.append(_softplus(z))
        if n % 4 == 3:
            yield
    rts = []
    for n, (r, hp) in enumerate(chains):
        spm = jnp.where(valids[r], sps[n], 0.0).astype(BF16)
        rts.append((jnp.dot(spm, u_ref[...], preferred_element_type=F32), spm))
        if n % 4 == 3:
            yield
    cmin = None
    for n, (r, hp) in enumerate(chains):
        cs = slice(hp * LANES, (hp + 1) * LANES)
        rt, spm = rts[n]
        a = jnp.where(valids[r], jnp.exp(zs[n] - sps[n] - rt), 0.0).astype(BF16)
        o_scrs[r][hp] = jnp.dot(a, v_ref[pl.ds(s0s[r], SB_WIN), cs], preferred_element_type=F32)
        c = jnp.broadcast_to(rt[:, :1] + spm[:, :1].astype(F32), (rt.shape[0], LANES))
        c_scrs[r][hp] = c
        cmin = c if cmin is None else jnp.minimum(cmin, c)
        if n % 2:
            yield
    emit(cmin)


def _sb_rest(q_refs, q0s, k_ref, v_ref, m_ref, in_head, o_scrs, c_scrs, cmin):
    npairs = q_refs[0].shape[1] // LANES
    nh = len(in_head)
    s0s = [jnp.maximum(q0 - (SB_WIN - SB_Q), 0) for q0 in q0s]
    col = lax.broadcasted_iota(jnp.int32, (nh * SB_Q, SB_KB), 1)

    def more(carry):
        d, cm = carry
        return (s0s[-1] - d * SB_KB > 0) & (cm < SB_DONE)

    def one_block(carry):
        d, _ = carry
        cmin = None
        for r in range(len(q_refs)):
            end = s0s[r] - d * SB_KB
            start = pl.multiple_of(jnp.maximum(end - SB_KB, 0), SB_Q)
            valid = col < (end - start)
            for hp in range(npairs):
                cs = slice(hp * LANES, (hp + 1) * LANES)
                qs = _head_rows(q_refs[r][:, cs], in_head)
                z = lax.dot_general(qs, k_ref[pl.ds(start, SB_KB), cs], _NT, preferred_element_type=F32)
                sp = _softplus(z)
                rt = jnp.dot(jnp.where(valid, sp, 0.0).astype(BF16), m_ref[...],
                             preferred_element_type=F32)
                c = c_scrs[r][hp]
                a = jnp.where(valid, jnp.exp(z - sp - rt[:, :SB_KB] - c), 0.0)
                o_scrs[r][hp] += jnp.dot(a.astype(BF16), v_ref[pl.ds(start, SB_KB), cs],
                                         preferred_element_type=F32)
                c = c + rt[:, SB_KB:]
                c_scrs[r][hp] = c
                cmin = c if cmin is None else jnp.minimum(cmin, c)
        return d + 1, jnp.min(cmin)

    lax.while_loop(more, one_block, (jnp.int32(0), jnp.min(cmin)))


def _sb_finish(o_scr, in_head):
    nh = len(in_head)
    parts = []
    for hp in range(o_scr.shape[0]):
        o2 = o_scr[hp]
        res = o2[(nh - 1) * SB_Q:]
        for h in range(nh - 1):
            res = jnp.where(in_head[h], o2[h * SB_Q:(h + 1) * SB_Q], res)
        parts.append(res)
    return jnp.concatenate(parts, axis=1)


def _tail_stages(x_ref, mix_in, p_ref, apn_ref, fpre_ref, fpost_ref, wout_ref, wg_ref, wu_ref,
                 wd_ref, pp_ref, pg_ref, act_scr, o_ref):
    d = wout_ref.shape[1]
    sw = d // TAIL_SLABS
    slabs = [slice(j * sw, (j + 1) * sw) for j in range(TAIL_SLABS)]
    parts = []
    for sl in slabs:
        parts.append(jnp.dot(mix_in, wout_ref[:, sl], preferred_element_type=F32))
        yield
    h1 = x_ref[...] + _rms(jnp.concatenate(parts, axis=1)) * apn_ref[...]
    u = (_rms(h1) * fpre_ref[...]).astype(BF16)
    for j in range(act_scr.shape[1] // FF_SLAB):
        fs = slice(j * FF_SLAB, (j + 1) * FF_SLAB)
        g = jnp.dot(u, wg_ref[:, fs], preferred_element_type=F32)
        up = jnp.dot(u, wu_ref[:, fs], preferred_element_type=F32)
        act_scr[:, fs] = (g * jax.nn.sigmoid(g) * up).astype(BF16)
        yield
    act = act_scr[...]
    parts = []
    for sl in slabs:
        parts.append(jnp.dot(act, wd_ref[:, sl], preferred_element_type=F32))
        yield
    h2 = h1 + _rms(jnp.concatenate(parts, axis=1)) * fpost_ref[...]
    emb = jnp.dot(p_ref[...].astype(BF16), pp_ref[...], preferred_element_type=F32)
    h2b = h2.astype(BF16)
    for sl in slabs:
        gate = jnp.dot(h2b, pg_ref[:, sl], preferred_element_type=F32)
        o_ref[:, sl] = h2[:, sl] + emb[:, sl] * jax.nn.sigmoid(gate)
        yield


def _round_robin(gens):
    live = list(gens)
    while live:
        for gen in list(live):
            try:
                next(gen)
                yield
            except StopIteration:
                live.remove(gen)


def _interleave(streams):
    state = [[gen, 0, n] for gen, n in streams]
    while state:
        cur = min(state, key=lambda t: t[1] / t[2])
        try:
            next(cur[0])
            cur[1] += 1
        except StopIteration:
            state.remove(cur)


def _fused_body(hq_ref, hf_ref, hi_ref, hg_ref, sq_ref, k_ref, v_ref, x_ref, p_ref,
                gam_ref, hgn_ref, tri_ref, esel_ref, u_ref, m_ref, sbn_ref,
                apn_ref, fpre_ref, fpost_ref, wout_ref, wg_ref, wu_ref, wd_ref, pp_ref, pg_ref,
                o_ref,
                mix_scr, st_ref, b_scr, lk_scr, k_scr, d_scr, so_scr, sc_scr, act_scr,
                *, layer, dh, blocks_per_seq):
    c = HG_CHUNK
    w = hq_ref.shape[1]
    nsub = hq_ref.shape[0] // c
    g = pl.program_id(0)
    last = pl.num_programs(0) - 1
    i = jnp.minimum(g, last - 1) % blocks_per_seq

    @pl.when(g % blocks_per_seq == 0)
    def _():
        st_ref[...] = jnp.zeros_like(st_ref)

    @pl.when(g == 0)
    def _():
        mix_scr[...] = jnp.zeros_like(mix_scr)

    slot = g % 2

    gam = gam_ref[...]
    e = jnp.exp(gam - jnp.max(gam, axis=0, keepdims=True))
    lb = jnp.sum(e[:layer + 1], axis=0, keepdims=True) / jnp.sum(e, axis=0, keepdims=True)

    def hgrn_emit(r):
        def emit(o):
            rows = pl.ds(r * c, c)
            gt = hg_ref[rows, :].astype(F32)
            mix_scr[slot, rows, 0:w] = (
                _rms(o) * hgn_ref[...] * (gt * jax.nn.sigmoid(gt))).astype(BF16)
        return emit

    hgrn = _round_robin([
        _hgrn_stages(hq_ref.at[pl.ds(r * c, c)], hf_ref.at[pl.ds(r * c, c)],
                     hi_ref.at[pl.ds(r * c, c)], lb, tri_ref, esel_ref, st_ref, b_scr.at[r],
                     lk_scr.at[r], k_scr.at[r], d_scr.at[r], hgrn_emit(r))
        for r in range(nsub)])

    in_head, col_minus_row = _sb_consts(dh)
    nq = hq_ref.shape[0] // SB_Q
    q_refs = [sq_ref.at[pl.ds(r * SB_Q, SB_Q)] for r in range(nq)]
    q0s = [(i * nq + r) * SB_Q for r in range(nq)]
    o_scrs = [so_scr.at[r] for r in range(nq)]
    c_scrs = [sc_scr.at[r] for r in range(nq)]
    cmins = []
    sb = _sb_stages(q_refs, q0s, k_ref, v_ref, u_ref, in_head, col_minus_row, o_scrs, c_scrs,
                    cmins.append)

    tail = _tail_stages(x_ref, mix_scr[1 - slot], p_ref, apn_ref, fpre_ref, fpost_ref, wout_ref,
                        wg_ref, wu_ref, wd_ref, pp_ref, pg_ref, act_scr, o_ref)
    n_tail = 3 * TAIL_SLABS + act_scr.shape[1] // FF_SLAB
    n_sb = 3 * (nq * (w // LANES) // 4) + nq * (w // LANES) // 2
    _interleave([(tail, n_tail), (hgrn, nsub * (2 + 3 * HG_HEADS)), (sb, n_sb)])

    _sb_rest(q_refs, q0s, k_ref, v_ref, m_ref, in_head, o_scrs, c_scrs, cmins[0])
    for r in range(nq):
        o = _sb_finish(o_scrs[r], in_head)
        mix_scr[slot, pl.ds(r * SB_Q, SB_Q), w:2 * w] = (_rms(o) * sbn_ref[...]).astype(BF16)


def _fused(hq, hf, hi, hg, sq, sk, sv, x2, p2, gamma, hgn, sbn, apn, fpre, fpost,
           wout, wg, wu, wd, pp, pg, *, layer, dh, tm):
    bsz, t, w = hq.shape
    n, d = x2.shape
    c = HG_CHUNK
    nsub = tm // c
    bps = t // tm
    nblk = n // tm
    tri = jnp.tril(jnp.ones((c, c), F32)).astype(BF16)
    sel = (jnp.arange(HG_SUB * LANES)[:, None] // LANES) == (jnp.arange(LANES)[None, :] % HG_SUB)
    esel = sel.astype(BF16)
    def later_and_ones(nk):
        later = jnp.arange(nk)[:, None] > jnp.arange(nk)[None, :]
        return jnp.concatenate([later, jnp.ones((nk, LANES), bool)], axis=1).astype(BF16)

    ucat, mcat = later_and_ones(SB_WIN)[:, :SB_WIN], later_and_ones(SB_KB)
    nh = LANES // dh
    nq = tm // SB_Q

    def mixer_idx(g):
        gm = jnp.minimum(g, nblk - 1)
        return (gm // bps, gm % bps, 0)

    mspec = pl.BlockSpec((None, tm, w), mixer_idx)
    kvspec = pl.BlockSpec((None, t, w), lambda g: (jnp.minimum(g, nblk - 1) // bps, 0, 0),
                          pipeline_mode=pl.Buffered(1))
    row = lambda a: pl.BlockSpec((tm, a.shape[1]), lambda g: (jnp.maximum(g - 1, 0), 0))
    consts = (gamma, hgn, tri, esel, ucat, mcat, sbn, apn, fpre, fpost, wout, wg, wu, wd, pp, pg)
    return pl.pallas_call(
        functools.partial(_fused_body, layer=layer, dh=dh, blocks_per_seq=bps),
        grid=(nblk + 1,),
        in_specs=[mspec] * 5 + [kvspec, kvspec, row(x2), row(p2)]
                 + [_const_spec(a.shape) for a in consts],
        out_specs=row(x2),
        out_shape=jax.ShapeDtypeStruct((n, d), F32),
        scratch_shapes=[
            pltpu.VMEM((2, tm, 2 * w), BF16),
            pltpu.VMEM((HG_HEADS, LANES, LANES), F32),
            pltpu.VMEM((nsub, HG_HEADS, c, LANES), F32),
            pltpu.VMEM((nsub, HG_HEADS, c, LANES), F32),
            pltpu.VMEM((nsub, HG_HEADS, c, LANES), F32),
            pltpu.VMEM((nsub, HG_HEADS, c, HG_SUB * LANES), BF16),
            pltpu.VMEM((nq, w // LANES, nh * SB_Q, LANES), F32),
            pltpu.VMEM((nq, w // LANES, nh * SB_Q, LANES), F32),
            pltpu.VMEM((tm, wd.shape[0]), BF16),
        ],
        compiler_params=pltpu.CompilerParams(
            dimension_semantics=("arbitrary",), vmem_limit_bytes=VMEM_LIMIT),
        name="mix_tail",
    )(hq, hf, hi, hg, sq, sk, sv, x2, p2, *consts)


def kernel(x, p, attn_pre_norm, w_in, hg_lower_gamma, hg_out_norm, sb_out_norm, w_out, attn_post_norm, ffn_pre_norm, w_gate_up, w_down, ffn_post_norm, ple_proj, ple_gate):
    bsz, t, d = x.shape
    depth = w_in.shape[0]
    n = bsz * t
    width = hg_out_norm.shape[1]
    assert sb_out_norm.shape[1] == width and w_in.shape[2] == 7 * width
    assert width == HG_HEADS * LANES and t >= SB_WIN
    dh = width // SB_HEADS
    d_ff = w_down.shape[1]
    assert d_ff % FF_SLAB == 0 and d % (TAIL_SLABS * LANES) == 0
    tm = 512
    tm_fused = 256
    assert n % tm == 0 and t % tm_fused == 0 and tm_fused % HG_CHUNK == 0 and tm_fused % SB_Q == 0

    h = x.reshape(n, d)
    for i in range(depth):
        hq, hf, hi, hg, sq, sk, sv = _inproj(
            h, attn_pre_norm[i][None], w_in[i].astype(BF16),
            width=width, sb_scale=dh ** -0.5, tm=tm)
        to3 = lambda a: a.reshape(bsz, t, width)
        wgu = w_gate_up[i].astype(BF16)
        h = _fused(
            to3(hq), to3(hf), to3(hi), to3(hg), to3(sq), to3(sk), to3(sv), h, p[i].reshape(n, -1),
            hg_lower_gamma, hg_out_norm[i][None], sb_out_norm[i][None],
            attn_post_norm[i][None], ffn_pre_norm[i][None], ffn_post_norm[i][None],
            w_out[i].astype(BF16), wgu[:, :d_ff], wgu[:, d_ff:], w_down[i].astype(BF16),
            ple_proj[i].astype(BF16), ple_gate[i].astype(BF16), layer=i, dh=dh, tm=tm_fused)
    return h.reshape(bsz, t, d)
```

```python
import functools

import jax
import jax.numpy as jnp
from jax import lax
from jax.experimental import pallas as pl
from jax.experimental.pallas import tpu as pltpu

F32 = jnp.float32
BF16 = jnp.bfloat16
EPS = 1e-6
LOG2E = 1.4426950408889634

HG_HEADS = 4
SB_HEADS = 8
LANES = 128
BF16_ROWS = 16
SQ_GROUP = 4
HG_CHUNK = 128
HG_SUB = 8
SB_Q = 64
SB_WIN = 256
SB_KB = 128
SB_DONE = 115.0
FF_SLAB = 256
TAIL_SLABS = 4
VMEM_LIMIT = 56 * 1024 * 1024

_NT = (((1,), (1,)), ((), ()))
_TN = (((0,), (0,)), ((), ()))


def _rms(xf):
    return xf * lax.rsqrt(jnp.mean(xf * xf, axis=-1, keepdims=True) + EPS)


def _softplus(z):
    return jnp.maximum(z, 0.0) + jnp.log(1.0 + jnp.exp2(jnp.abs(z) * -LOG2E))


def _logaddexp(a, c):
    return jnp.maximum(a, c) + jnp.log(1.0 + jnp.exp(-jnp.abs(a - c)))


def _const_spec(shape):
    nd = len(shape)
    return pl.BlockSpec(shape, lambda *_: (0,) * nd, pipeline_mode=pl.Buffered(1))


def _inproj_body(x_ref, nw_ref, w_ref, *refs, width, sb_scale, n_proj, cast_splits):
    f32_refs = refs[:len(cast_splits)]
    outs = refs[len(cast_splits):]
    u = (_rms(x_ref[...]) * nw_ref[...]).astype(BF16)
    for j, o_ref in enumerate(outs[:n_proj]):
        r = jnp.dot(u, w_ref[:, j * width:(j + 1) * width], preferred_element_type=F32)
        if j == SQ_GROUP:
            r = r * sb_scale
        o_ref[...] = r.astype(o_ref.dtype)

    dst = list(outs[n_proj:])
    for src_ref, parts in zip(f32_refs, cast_splits):
        blk = src_ref[...].astype(BF16)
        cw = blk.shape[1] // parts
        for k in range(parts):
            dst.pop(0)[...] = blk[:, k * cw:(k + 1) * cw]


def _cast_plan(rows, steps):
    for nb in range(steps, 0, -1):
        if steps % nb == 0 and rows % nb == 0 and (rows // nb) % BF16_ROWS == 0:
            return rows // nb, steps // nb
    raise ValueError((rows, steps))


def _inproj(x2, nw, w_in, casts, *, width, sb_scale, tm):
    n, d = x2.shape
    steps = n // tm
    out_dtypes = (BF16, F32, BF16, BF16, BF16, BF16, BF16)
    in_specs = [pl.BlockSpec((tm, d), lambda i: (i, 0)), _const_spec((1, d)), _const_spec(w_in.shape)]
    out_specs = [pl.BlockSpec((tm, width), lambda i: (i, 0)) for _ in out_dtypes]
    out_shape = [jax.ShapeDtypeStruct((n, width), dt) for dt in out_dtypes]
    for wgt, parts in casts:
        rows, cols = wgt.shape
        rb, reps = _cast_plan(rows, steps)
        idx = functools.partial(lambda i, reps: (i // reps, 0), reps=reps)
        in_specs.append(pl.BlockSpec((rb, cols), idx))
        out_specs += [pl.BlockSpec((rb, cols // parts), idx)] * parts
        out_shape += [jax.ShapeDtypeStruct((rows, cols // parts), BF16)] * parts
    return pl.pallas_call(
        functools.partial(_inproj_body, width=width, sb_scale=sb_scale, n_proj=len(out_dtypes),
                          cast_splits=tuple(parts for _, parts in casts)),
        grid=(steps,),
        in_specs=in_specs,
        out_specs=out_specs,
        out_shape=out_shape,
        compiler_params=pltpu.CompilerParams(
            dimension_semantics=("arbitrary",), vmem_limit_bytes=VMEM_LIMIT),
        name="inproj",
    )(x2, nw, w_in, *[wgt for wgt, _ in casts])


def _hgrn_stages(hq_ref, hf_ref, hi_ref, lb, tri_ref, esel_ref, st_ref, b_scr, lk_scr, k_scr, d_scr,
                 emit):
    c = HG_CHUNK
    sub = HG_SUB

    qz = hq_ref[...].astype(F32)
    q = qz * jax.nn.sigmoid(qz)
    z = hf_ref[...]
    log_sig = jnp.minimum(z, 0.0) - jnp.log(1.0 + jnp.exp2(jnp.abs(z) * -LOG2E))
    log_1mlb = jnp.log1p(-lb)
    logf = _logaddexp(jnp.log(lb), log_1mlb + log_sig) * LOG2E
    logk = (log_1mlb + log_sig - z) * LOG2E
    lhi = logf.astype(BF16)
    llo = (logf - lhi.astype(F32)).astype(BF16)
    tri = tri_ref[...]
    b = (jnp.dot(tri, lhi, preferred_element_type=F32)
         + jnp.dot(tri, llo, preferred_element_type=F32))
    lk = logk - b
    for h in range(HG_HEADS):
        cs = slice(h * LANES, (h + 1) * LANES)
        b_scr[h] = b[:, cs]
        lk_scr[h] = lk[:, cs]
        k_scr[h] = logk[:, cs]
    yield

    two = 2 * sub
    for h in range(HG_HEADS):
        cs = slice(h * LANES, (h + 1) * LANES)
        for r0 in range(0, c, two):
            bi = b[r0:r0 + two, cs]
            qi = q[r0:r0 + two, cs]
            for s in range(sub):
                def key_row(ref):
                    return jnp.concatenate([ref[h, pl.ds(r0 + s, sub, stride=0), :],
                                            ref[h, pl.ds(r0 + sub + s, sub, stride=0), :]], axis=0)
                d = qi * jnp.exp2(jnp.minimum(bi + key_row(lk_scr), key_row(k_scr)))
                d_scr[h, r0:r0 + two, s * LANES:(s + 1) * LANES] = d.astype(BF16)
            if (r0 // two) % 4 == 3:
                yield

    row = lax.broadcasted_iota(jnp.int32, (c, LANES), 0)
    col = lax.broadcasted_iota(jnp.int32, (c, LANES), 1)
    keep_diag = ((col // sub) == (row // sub)) & ((col % sub) <= (row % sub))
    half = c // 2
    rowq = lax.broadcasted_iota(jnp.int32, (half, LANES), 0)
    colq = lax.broadcasted_iota(jnp.int32, (half, LANES), 1)
    levels = []
    span = half
    while span >= sub:
        levels.append((span, ((colq // (2 * span)) == (rowq // span)) & ((colq % (2 * span)) < span)))
        span //= 2

    outs = []
    for h in range(HG_HEADS):
        cs = slice(h * LANES, (h + 1) * LANES)
        b_h, q_h, lk_h = b[:, cs], q[:, cs], lk[:, cs]
        v = hi_ref[:, cs]
        b_last = b_scr[h, c - 1:c, :]

        st = st_ref[h]
        o_h = lax.dot_general((q_h * jnp.exp2(b_h)).astype(BF16), st.astype(BF16), _NT,
                              preferred_element_type=F32)
        kh = jnp.exp2(lk_h + b_last).astype(BF16)
        st_ref[h] = st * jnp.exp2(b_last) + lax.dot_general(
            v, kh, _TN, preferred_element_type=F32)

        sdiag = jnp.where(keep_diag, jnp.dot(d_scr[h], esel_ref[...], preferred_element_type=F32), 0.0)
        pieces = [sdiag[i:i + 8] for i in range(0, c, 8)]

        for span, keep in levels:
            qparts, kparts = [], []
            for lo in range(0, c, 2 * span):
                mid, hi = lo + span, lo + 2 * span
                r = b_scr[h, mid - 1:mid, :]
                qparts.append(q_h[mid:hi] * jnp.exp2(b_h[mid:hi] - r))
                kparts.append(jnp.exp2(lk_h[lo:mid] + r))
                kparts.append(jnp.zeros((span, LANES), F32))
            sc = lax.dot_general(jnp.concatenate(qparts, axis=0).astype(BF16),
                                 jnp.concatenate(kparts, axis=0).astype(BF16), _NT,
                                 preferred_element_type=F32)
            sc = jnp.where(keep, sc, 0.0)
            for g, lo in enumerate(range(0, c, 2 * span)):
                for m in range(span // 8):
                    idx = (lo + span) // 8 + m
                    pieces[idx] = pieces[idx] + sc[g * span + 8 * m:g * span + 8 * m + 8]
        scores = jnp.concatenate(pieces, axis=0).astype(BF16)
        outs.append(o_h + jnp.dot(scores, v, preferred_element_type=F32))
        yield
    emit(jnp.concatenate(outs, axis=1))


def _sb_consts(dh):
    nh = LANES // dh
    lane = lax.broadcasted_iota(jnp.int32, (SB_Q, LANES), 1)
    in_head = [(lane >= h * dh) & (lane < (h + 1) * dh) for h in range(nh)]
    row = lax.broadcasted_iota(jnp.int32, (nh * SB_Q, SB_WIN), 0) & (SB_Q - 1)
    col = lax.broadcasted_iota(jnp.int32, (nh * SB_Q, SB_WIN), 1)
    return in_head, col - row


def _head_rows(q, in_head):
    return jnp.concatenate([jnp.where(m, q, jnp.zeros_like(q)) for m in in_head], axis=0)


def _sb_stages(q_refs, q0s, k_ref, v_ref, u_ref, in_head, col_minus_row, o_scrs, c_scrs, emit):
    npairs = q_refs[0].shape[1] // LANES
    chains = [(r, hp) for r in range(len(q_refs)) for hp in range(npairs)]
    s0s = [pl.multiple_of(jnp.maximum(q0 - (SB_WIN - SB_Q), 0), SB_Q) for q0 in q0s]
    valids = [col_minus_row < (q0 - s0) for q0, s0 in zip(q0s, s0s)]
    zs = []
    for n, (r, hp) in enumerate(chains):
        cs = slice(hp * LANES, (hp + 1) * LANES)
        zs.append(lax.dot_general(_head_rows(q_refs[r][:, cs], in_head),
                                  k_ref[pl.ds(s0s[r], SB_WIN), cs], _NT,
                                  preferred_element_type=F32))
        if n % 4 == 3:
            yield
    sps = []
    for n, z in enumerate(zs):
        sps.append(_softplus(z))
        if n % 4 == 3:
            yield
    rts = []
    for n, (r, hp) in enumerate(chains):
        spm = jnp.where(valids[r], sps[n], 0.0).astype(BF16)
        rts.append((jnp.dot(spm, u_ref[...], preferred_element_type=F32), spm))
        if n % 4 == 3:
            yield
    cmin = None
    for n, (r, hp) in enumerate(chains):
        cs = slice(hp * LANES, (hp + 1) * LANES)
        rt, spm = rts[n]
        a = jnp.where(valids[r], jnp.exp(zs[n] - sps[n] - rt), 0.0).astype(BF16)
        o_scrs[r][hp] = jnp.dot(a, v_ref[pl.ds(s0s[r], SB_WIN), cs], preferred_element_type=F32)
        c = jnp.broadcast_to(rt[:, :1] + spm[:, :1].astype(F32), (rt.shape[0], LANES))
        c_scrs[r][hp] = c
        cmin = c if cmin is None else jnp.minimum(cmin, c)
        if n % 2:
            yield
    emit(cmin)


def _sb_rest(q_refs, q0s, k_ref, v_ref, m_ref, in_head, o_scrs, c_scrs, cmin):
    npairs = q_refs[0].shape[1] // LANES
    nh = len(in_head)
    s0s = [jnp.maximum(q0 - (SB_WIN - SB_Q), 0) for q0 in q0s]
    col = lax.broadcasted_iota(jnp.int32, (nh * SB_Q, SB_KB), 1)

    def more(carry):
        d, cm = carry
        return (s0s[-1] - d * SB_KB > 0) & (cm < SB_DONE)

    def one_block(carry):
        d, _ = carry
        cmin = None
        for r in range(len(q_refs)):
            end = s0s[r] - d * SB_KB
            start = pl.multiple_of(jnp.maximum(end - SB_KB, 0), SB_Q)
            valid = col < (end - start)
            for hp in range(npairs):
                cs = slice(hp * LANES, (hp + 1) * LANES)
                qs = _head_rows(q_refs[r][:, cs], in_head)
                z = lax.dot_general(qs, k_ref[pl.ds(start, SB_KB), cs], _NT, preferred_element_type=F32)
                sp = _softplus(z)
                rt = jnp.dot(jnp.where(valid, sp, 0.0).astype(BF16), m_ref[...],
                             preferred_element_type=F32)
                c = c_scrs[r][hp]
                a = jnp.where(valid, jnp.exp(z - sp - rt[:, :SB_KB] - c), 0.0)
                o_scrs[r][hp] += jnp.dot(a.astype(BF16), v_ref[pl.ds(start, SB_KB), cs],
                                         preferred_element_type=F32)
                c = c + rt[:, SB_KB:]
                c_scrs[r][hp] = c
                cmin = c if cmin is None else jnp.minimum(cmin, c)
        return d + 1, jnp.min(cmin)

    lax.while_loop(more, one_block, (jnp.int32(0), jnp.min(cmin)))


def _sb_finish(o_scr, in_head):
    nh = len(in_head)
    parts = []
    for hp in range(o_scr.shape[0]):
        o2 = o_scr[hp]
        res = o2[(nh - 1) * SB_Q:]
        for h in range(nh - 1):
            res = jnp.where(in_head[h], o2[h * SB_Q:(h + 1) * SB_Q], res)
        parts.append(res)
    return jnp.concatenate(parts, axis=1)


def _tail_stages(x_ref, mix_in, p_ref, apn_ref, fpre_ref, fpost_ref, wout_ref, wg_ref, wu_ref,
                 wd_ref, pp_ref, pg_ref, act_scr, o_ref):
    d = wout_ref.shape[1]
    sw = d // TAIL_SLABS
    slabs = [slice(j * sw, (j + 1) * sw) for j in range(TAIL_SLABS)]
    parts = []
    for sl in slabs:
        parts.append(jnp.dot(mix_in, wout_ref[:, sl], preferred_element_type=F32))
        yield
    h1 = x_ref[...] + _rms(jnp.concatenate(parts, axis=1)) * apn_ref[...]
    u = (_rms(h1) * fpre_ref[...]).astype(BF16)
    for j in range(act_scr.shape[1] // FF_SLAB):
        fs = slice(j * FF_SLAB, (j + 1) * FF_SLAB)
        g = jnp.dot(u, wg_ref[:, fs], preferred_element_type=F32)
        up = jnp.dot(u, wu_ref[:, fs], preferred_element_type=F32)
        act_scr[:, fs] = (g * jax.nn.sigmoid(g) * up).astype(BF16)
        yield
    act = act_scr[...]
    parts = []
    for sl in slabs:
        parts.append(jnp.dot(act, wd_ref[:, sl], preferred_element_type=F32))
        yield
    h2 = h1 + _rms(jnp.concatenate(parts, axis=1)) * fpost_ref[...]
    emb = jnp.dot(p_ref[...].astype(BF16), pp_ref[...], preferred_element_type=F32)
    h2b = h2.astype(BF16)
    for sl in slabs:
        gate = jnp.dot(h2b, pg_ref[:, sl], preferred_element_type=F32)
        o_ref[:, sl] = h2[:, sl] + emb[:, sl] * jax.nn.sigmoid(gate)
        yield


def _round_robin(gens):
    live = list(gens)
    while live:
        for gen in list(live):
            try:
                next(gen)
                yield
            except StopIteration:
                live.remove(gen)


def _interleave(streams):
    state = [[gen, 0, n] for gen, n in streams]
    while state:
        cur = min(state, key=lambda t: t[1] / t[2])
        try:
            next(cur[0])
            cur[1] += 1
        except StopIteration:
            state.remove(cur)


def _fused_body(hq_ref, hf_ref, hi_ref, hg_ref, sq_ref, k_ref, v_ref, x_ref, p_ref,
                gam_ref, hgn_ref, tri_ref, esel_ref, u_ref, m_ref, sbn_ref,
                apn_ref, fpre_ref, fpost_ref, wout_ref, wg_ref, wu_ref, wd_ref, pp_ref, pg_ref,
                o_ref,
                mix_scr, st_ref, b_scr, lk_scr, k_scr, d_scr, so_scr, sc_scr, act_scr,
                *, layer, dh, blocks_per_seq):
    c = HG_CHUNK
    w = hq_ref.shape[1]
    nsub = hq_ref.shape[0] // c
    g = pl.program_id(0)
    last = pl.num_programs(0) - 1
    i = jnp.minimum(g, last - 1) % blocks_per_seq

    @pl.when(g % blocks_per_seq == 0)
    def _():
        st_ref[...] = jnp.zeros_like(st_ref)

    @pl.when(g == 0)
    def _():
        mix_scr[...] = jnp.zeros_like(mix_scr)

    slot = g % 2

    gam = gam_ref[...]
    e = jnp.exp(gam - jnp.max(gam, axis=0, keepdims=True))
    lb = jnp.sum(e[:layer + 1], axis=0, keepdims=True) / jnp.sum(e, axis=0, keepdims=True)

    def hgrn_emit(r):
        def emit(o):
            rows = pl.ds(r * c, c)
            gt = hg_ref[rows, :].astype(F32)
            mix_scr[slot, rows, 0:w] = (
                _rms(o) * hgn_ref[...] * (gt * jax.nn.sigmoid(gt))).astype(BF16)
        return emit

    hgrn = _round_robin([
        _hgrn_stages(hq_ref.at[pl.ds(r * c, c)], hf_ref.at[pl.ds(r * c, c)],
                     hi_ref.at[pl.ds(r * c, c)], lb, tri_ref, esel_ref, st_ref, b_scr.at[r],
                     lk_scr.at[r], k_scr.at[r], d_scr.at[r], hgrn_emit(r))
        for r in range(nsub)])

    in_head, col_minus_row = _sb_consts(dh)
    nq = hq_ref.shape[0] // SB_Q
    q_refs = [sq_ref.at[pl.ds(r * SB_Q, SB_Q)] for r in range(nq)]
    q0s = [(i * nq + r) * SB_Q for r in range(nq)]
    o_scrs = [so_scr.at[r] for r in range(nq)]
    c_scrs = [sc_scr.at[r] for r in range(nq)]
    cmins = []
    sb = _sb_stages(q_refs, q0s, k_ref, v_ref, u_ref, in_head, col_minus_row, o_scrs, c_scrs,
                    cmins.append)

    tail = _tail_stages(x_ref, mix_scr[1 - slot], p_ref, apn_ref, fpre_ref, fpost_ref, wout_ref,
                        wg_ref, wu_ref, wd_ref, pp_ref, pg_ref, act_scr, o_ref)
    n_tail = 3 * TAIL_SLABS + act_scr.shape[1] // FF_SLAB
    n_sb = 3 * (nq * (w // LANES) // 4) + nq * (w // LANES) // 2
    _interleave([(tail, n_tail), (hgrn, nsub * (2 + 3 * HG_HEADS)), (sb, n_sb)])

    _sb_rest(q_refs, q0s, k_ref, v_ref, m_ref, in_head, o_scrs, c_scrs, cmins[0])
    for r in range(nq):
        o = _sb_finish(o_scrs[r], in_head)
        mix_scr[slot, pl.ds(r * SB_Q, SB_Q), w:2 * w] = (_rms(o) * sbn_ref[...]).astype(BF16)


def _fused(hq, hf, hi, hg, sq, sk, sv, x2, p2, gamma, hgn, sbn, apn, fpre, fpost,
           wout, wg, wu, wd, pp, pg, *, layer, dh, tm):
    bsz, t, w = hq.shape
    n, d = x2.shape
    c = HG_CHUNK
    nsub = tm // c
    bps = t // tm
    nblk = n // tm
    tri = jnp.tril(jnp.ones((c, c), F32)).astype(BF16)
    sel = (jnp.arange(HG_SUB * LANES)[:, None] // LANES) == (jnp.arange(LANES)[None, :] % HG_SUB)
    esel = sel.astype(BF16)
    def later_and_ones(nk):
        later = jnp.arange(nk)[:, None] > jnp.arange(nk)[None, :]
        return jnp.concatenate([later, jnp.ones((nk, LANES), bool)], axis=1).astype(BF16)

    ucat, mcat = later_and_ones(SB_WIN)[:, :SB_WIN], later_and_ones(SB_KB)
    nh = LANES // dh
    nq = tm // SB_Q

    def mixer_idx(g):
        gm = jnp.minimum(g, nblk - 1)
        return (gm // bps, gm % bps, 0)

    mspec = pl.BlockSpec((None, tm, w), mixer_idx)
    kvspec = pl.BlockSpec((None, t, w), lambda g: (jnp.minimum(g, nblk - 1) // bps, 0, 0),
                          pipeline_mode=pl.Buffered(1))
    row = lambda a: pl.BlockSpec((tm, a.shape[1]), lambda g: (jnp.maximum(g - 1, 0), 0))
    consts = (gamma, hgn, tri, esel, ucat, mcat, sbn, apn, fpre, fpost, wout, wg, wu, wd, pp, pg)
    return pl.pallas_call(
        functools.partial(_fused_body, layer=layer, dh=dh, blocks_per_seq=bps),
        grid=(nblk + 1,),
        in_specs=[mspec] * 5 + [kvspec, kvspec, row(x2), row(p2)]
                 + [_const_spec(a.shape) for a in consts],
        out_specs=row(x2),
        out_shape=jax.ShapeDtypeStruct((n, d), F32),
        scratch_shapes=[
            pltpu.VMEM((2, tm, 2 * w), BF16),
            pltpu.VMEM((HG_HEADS, LANES, LANES), F32),
            pltpu.VMEM((nsub, HG_HEADS, c, LANES), F32),
            pltpu.VMEM((nsub, HG_HEADS, c, LANES), F32),
            pltpu.VMEM((nsub, HG_HEADS, c, LANES), F32),
            pltpu.VMEM((nsub, HG_HEADS, c, HG_SUB * LANES), BF16),
            pltpu.VMEM((nq, w // LANES, nh * SB_Q, LANES), F32),
            pltpu.VMEM((nq, w // LANES, nh * SB_Q, LANES), F32),
            pltpu.VMEM((tm, wd.shape[0]), BF16),
        ],
        compiler_params=pltpu.CompilerParams(
            dimension_semantics=("arbitrary",), vmem_limit_bytes=VMEM_LIMIT),
        name="mix_tail",
    )(hq, hf, hi, hg, sq, sk, sv, x2, p2, *consts)


def kernel(x, p, attn_pre_norm, w_in, hg_lower_gamma, hg_out_norm, sb_out_norm, w_out, attn_post_norm, ffn_pre_norm, w_gate_up, w_down, ffn_post_norm, ple_proj, ple_gate):
    bsz, t, d = x.shape
    depth = w_in.shape[0]
    n = bsz * t
    width = hg_out_norm.shape[1]
    assert sb_out_norm.shape[1] == width and w_in.shape[2] == 7 * width
    assert width == HG_HEADS * LANES and t >= SB_WIN
    dh = width // SB_HEADS
    d_ff = w_down.shape[1]
    assert d_ff % FF_SLAB == 0 and d % (TAIL_SLABS * LANES) == 0
    tm = 512
    tm_fused = 256
    assert n % tm == 0 and t % tm_fused == 0 and tm_fused % HG_CHUNK == 0 and tm_fused % SB_Q == 0

    h = x.reshape(n, d)
    for i in range(depth):
        casts = [(w_out[i], 1), (w_gate_up[i], 2), (w_down[i], 1), (ple_proj[i], 1), (ple_gate[i], 1)]
        hq, hf, hi, hg, sq, sk, sv, wout, wg, wu, wd, pp, pg = _inproj(
            h, attn_pre_norm[i][None], w_in[i].astype(BF16), casts,
            width=width, sb_scale=dh ** -0.5, tm=tm)
        to3 = lambda a: a.reshape(bsz, t, width)
        h = _fused(
            to3(hq), to3(hf), to3(hi), to3(hg), to3(sq), to3(sk), to3(sv), h, p[i].reshape(n, -1),
            hg_lower_gamma, hg_out_norm[i][None], sb_out_norm[i][None],
            attn_post_norm[i][None], ffn_pre_norm[i][None], ffn_post_norm[i][None],
            wout, wg, wu, wd, pp, pg, layer=i, dh=dh, tm=tm_fused)
    return h.reshape(bsz, t, d)
```

```python
import functools

import jax
import jax.numpy as jnp
from jax import lax
from jax.experimental import pallas as pl
from jax.experimental.pallas import tpu as pltpu

F32 = jnp.float32
BF16 = jnp.bfloat16
EPS = 1e-6
LOG2E = 1.4426950408889634

HG_HEADS = 4
SB_HEADS = 8
LANES = 128
BF16_ROWS = 16
SQ_GROUP = 4
HG_CHUNK = 128
HG_SUB = 8
SB_Q = 64
SB_WIN = 256
SB_KB = 128
SB_DONE = 115.0
FF_SLAB = 256
TAIL_SLABS = 4
VMEM_LIMIT = 56 * 1024 * 1024

_NT = (((1,), (1,)), ((), ()))
_TN = (((0,), (0,)), ((), ()))


def _rms(xf):
    return xf * lax.rsqrt(jnp.mean(xf * xf, axis=-1, keepdims=True) + EPS)


def _softplus(z):
    return jnp.maximum(z, 0.0) + jnp.log(1.0 + jnp.exp2(jnp.abs(z) * -LOG2E))


def _logaddexp(a, c):
    return jnp.maximum(a, c) + jnp.log(1.0 + jnp.exp(-jnp.abs(a - c)))


def _const_spec(shape):
    nd = len(shape)
    return pl.BlockSpec(shape, lambda *_: (0,) * nd, pipeline_mode=pl.Buffered(1))


def _inproj_body(x_ref, nw_ref, w_ref, *refs, width, sb_scale, n_proj, cast_splits):
    f32_refs = refs[:len(cast_splits)]
    outs = refs[len(cast_splits):]
    u = (_rms(x_ref[...]) * nw_ref[...]).astype(BF16)
    for j, o_ref in enumerate(outs[:n_proj]):
        r = jnp.dot(u, w_ref[:, j * width:(j + 1) * width], preferred_element_type=F32)
        if j == SQ_GROUP:
            r = r * sb_scale
        o_ref[...] = r.astype(o_ref.dtype)

    dst = list(outs[n_proj:])
    for src_ref, parts in zip(f32_refs, cast_splits):
        blk = src_ref[...].astype(BF16)
        cw = blk.shape[1] // parts
        for k in range(parts):
            dst.pop(0)[...] = blk[:, k * cw:(k + 1) * cw]


def _cast_plan(rows, steps):
    for nb in range(steps, 0, -1):
        if steps % nb == 0 and rows % nb == 0 and (rows // nb) % BF16_ROWS == 0:
            return rows // nb, steps // nb
    raise ValueError((rows, steps))


def _inproj(x2, nw, w_in, casts, *, width, sb_scale, tm):
    n, d = x2.shape
    steps = n // tm
    out_dtypes = (BF16, F32, BF16, BF16, BF16, BF16, BF16)
    in_specs = [pl.BlockSpec((tm, d), lambda i: (i, 0)), _const_spec((1, d)), _const_spec(w_in.shape)]
    out_specs = [pl.BlockSpec((tm, width), lambda i: (i, 0)) for _ in out_dtypes]
    out_shape = [jax.ShapeDtypeStruct((n, width), dt) for dt in out_dtypes]
    for wgt, parts in casts:
        rows, cols = wgt.shape
        rb, reps = _cast_plan(rows, steps)
        idx = functools.partial(lambda i, reps: (i // reps, 0), reps=reps)
        in_specs.append(pl.BlockSpec((rb, cols), idx))
        out_specs += [pl.BlockSpec((rb, cols // parts), idx)] * parts
        out_shape += [jax.ShapeDtypeStruct((rows, cols // parts), BF16)] * parts
    return pl.pallas_call(
        functools.partial(_inproj_body, width=width, sb_scale=sb_scale, n_proj=len(out_dtypes),
                          cast_splits=tuple(parts for _, parts in casts)),
        grid=(steps,),
        in_specs=in_specs,
        out_specs=out_specs,
        out_shape=out_shape,
        compiler_params=pltpu.CompilerParams(
            dimension_semantics=("arbitrary",), vmem_limit_bytes=VMEM_LIMIT),
        name="inproj",
    )(x2, nw, w_in, *[wgt for wgt, _ in casts])


def _hgrn_stages(hq_ref, hf_ref, hi_ref, lb, tri_ref, esel_ref, st_ref, b_scr, lk_scr, k_scr, d_scr,
                 emit):
    c = HG_CHUNK
    sub = HG_SUB

    qz = hq_ref[...].astype(F32)
    q = qz * jax.nn.sigmoid(qz)
    z = hf_ref[...]
    log_sig = jnp.minimum(z, 0.0) - jnp.log(1.0 + jnp.exp2(jnp.abs(z) * -LOG2E))
    log_1mlb = jnp.log1p(-lb)
    logf = _logaddexp(jnp.log(lb), log_1mlb + log_sig) * LOG2E
    logk = (log_1mlb + log_sig - z) * LOG2E
    lhi = logf.astype(BF16)
    llo = (logf - lhi.astype(F32)).astype(BF16)
    tri = tri_ref[...]
    b = (jnp.dot(tri, lhi, preferred_element_type=F32)
         + jnp.dot(tri, llo, preferred_element_type=F32))
    lk = logk - b
    for h in range(HG_HEADS):
        cs = slice(h * LANES, (h + 1) * LANES)
        b_scr[h] = b[:, cs]
        lk_scr[h] = lk[:, cs]
        k_scr[h] = logk[:, cs]
    yield

    two = 2 * sub
    for h in range(HG_HEADS):
        cs = slice(h * LANES, (h + 1) * LANES)
        for r0 in range(0, c, two):
            bi = b[r0:r0 + two, cs]
            qi = q[r0:r0 + two, cs]
            for s in range(sub):
                def key_row(ref):
                    return jnp.concatenate([ref[h, pl.ds(r0 + s, sub, stride=0), :],
                                            ref[h, pl.ds(r0 + sub + s, sub, stride=0), :]], axis=0)
                d = qi * jnp.exp2(jnp.minimum(bi + key_row(lk_scr), key_row(k_scr)))
                d_scr[h, r0:r0 + two, s * LANES:(s + 1) * LANES] = d.astype(BF16)
            if (r0 // two) % 4 == 3:
                yield

    row = lax.broadcasted_iota(jnp.int32, (c, LANES), 0)
    col = lax.broadcasted_iota(jnp.int32, (c, LANES), 1)
    keep_diag = ((col // sub) == (row // sub)) & ((col % sub) <= (row % sub))
    half = c // 2
    rowq = lax.broadcasted_iota(jnp.int32, (half, LANES), 0)
    colq = lax.broadcasted_iota(jnp.int32, (half, LANES), 1)
    levels = []
    span = half
    while span >= sub:
        levels.append((span, ((colq // (2 * span)) == (rowq // span)) & ((colq % (2 * span)) < span)))
        span //= 2

    outs = []
    for h in range(HG_HEADS):
        cs = slice(h * LANES, (h + 1) * LANES)
        b_h, q_h, lk_h = b[:, cs], q[:, cs], lk[:, cs]
        v = hi_ref[:, cs]
        b_last = b_scr[h, c - 1:c, :]

        st = st_ref[h]
        o_h = lax.dot_general((q_h * jnp.exp2(b_h)).astype(BF16), st.astype(BF16), _NT,
                              preferred_element_type=F32)
        kh = jnp.exp2(lk_h + b_last).astype(BF16)
        st_ref[h] = st * jnp.exp2(b_last) + lax.dot_general(
            v, kh, _TN, preferred_element_type=F32)

        sdiag = jnp.where(keep_diag, jnp.dot(d_scr[h], esel_ref[...], preferred_element_type=F32), 0.0)
        pieces = [sdiag[i:i + 8] for i in range(0, c, 8)]

        for span, keep in levels:
            qparts, kparts = [], []
            for lo in range(0, c, 2 * span):
                mid, hi = lo + span, lo + 2 * span
                r = b_scr[h, mid - 1:mid, :]
                qparts.append(q_h[mid:hi] * jnp.exp2(b_h[mid:hi] - r))
                kparts.append(jnp.exp2(lk_h[lo:mid] + r))
                kparts.append(jnp.zeros((span, LANES), F32))
            sc = lax.dot_general(jnp.concatenate(qparts, axis=0).astype(BF16),
                                 jnp.concatenate(kparts, axis=0).astype(BF16), _NT,
                                 preferred_element_type=F32)
            sc = jnp.where(keep, sc, 0.0)
            for g, lo in enumerate(range(0, c, 2 * span)):
                for m in range(span // 8):
                    idx = (lo + span) // 8 + m
                    pieces[idx] = pieces[idx] + sc[g * span + 8 * m:g * span + 8 * m + 8]
        scores = jnp.concatenate(pieces, axis=0).astype(BF16)
        outs.append(o_h + jnp.dot(scores, v, preferred_element_type=F32))
        yield
    emit(jnp.concatenate(outs, axis=1))


def _sb_consts(dh):
    nh = LANES // dh
    lane = lax.broadcasted_iota(jnp.int32, (SB_Q, LANES), 1)
    in_head = [(lane >= h * dh) & (lane < (h + 1) * dh) for h in range(nh)]
    row = lax.broadcasted_iota(jnp.int32, (nh * SB_Q, SB_WIN), 0) & (SB_Q - 1)
    col = lax.broadcasted_iota(jnp.int32, (nh * SB_Q, SB_WIN), 1)
    return in_head, col - row


def _head_rows(q, in_head):
    return jnp.concatenate([jnp.where(m, q, jnp.zeros_like(q)) for m in in_head], axis=0)


def _sb_stages(q_refs, q0s, k_ref, v_ref, u_ref, in_head, col_minus_row, o_scrs, c_scrs, emit):
    npairs = q_refs[0].shape[1] // LANES
    chains = [(r, hp) for r in range(len(q_refs)) for hp in range(npairs)]
    s0s = [pl.multiple_of(jnp.maximum(q0 - (SB_WIN - SB_Q), 0), SB_Q) for q0 in q0s]
    valids = [col_minus_row < (q0 - s0) for q0, s0 in zip(q0s, s0s)]
    zs = []
    for n, (r, hp) in enumerate(chains):
        cs = slice(hp * LANES, (hp + 1) * LANES)
        zs.append(lax.dot_general(_head_rows(q_refs[r][:, cs], in_head),
                                  k_ref[pl.ds(s0s[r], SB_WIN), cs], _NT,
                                  preferred_element_type=F32))
        if n % 4 == 3:
            yield
    sps = []
    for n, z in enumerate(zs):
        sps.append(_softplus(z))
        if n % 4 == 3:
            yield
    rts = []
    for n, (r, hp) in enumerate(chains):
        spm = jnp.where(valids[r], sps[n], 0.0).astype(BF16)
        rts.append((jnp.dot(spm, u_ref[...], preferred_element_type=F32), spm))
        if n % 4 == 3:
            yield
    cmin = None
    for n, (r, hp) in enumerate(chains):
        cs = slice(hp * LANES, (hp + 1) * LANES)
        rt, spm = rts[n]
        a = jnp.where(valids[r], jnp.exp(zs[n] - sps[n] - rt), 0.0).astype(BF16)
        o_scrs[r][hp] = jnp.dot(a, v_ref[pl.ds(s0s[r], SB_WIN), cs], preferred_element_type=F32)
        c = jnp.broadcast_to(rt[:, :1] + spm[:, :1].astype(F32), (rt.shape[0], LANES))
        c_scrs[r][hp] = c
        cmin = c if cmin is None else jnp.minimum(cmin, c)
        if n % 2:
            yield
    emit(cmin)


def _sb_rest(q_refs, q0s, k_ref, v_ref, m_ref, in_head, o_scrs, c_scrs, cmin):
    npairs = q_refs[0].shape[1] // LANES
    nh = len(in_head)
    s0s = [jnp.maximum(q0 - (SB_WIN - SB_Q), 0) for q0 in q0s]
    col = lax.broadcasted_iota(jnp.int32, (nh * SB_Q, SB_KB), 1)

    def more(carry):
        d, cm = carry
        return (s0s[-1] - d * SB_KB > 0) & (cm < SB_DONE)

    def one_block(carry):
        d, _ = carry
        cmin = None
        for r in range(len(q_refs)):
            end = s0s[r] - d * SB_KB
            start = pl.multiple_of(jnp.maximum(end - SB_KB, 0), SB_Q)
            valid = col < (end - start)
            for hp in range(npairs):
                cs = slice(hp * LANES, (hp + 1) * LANES)
                qs = _head_rows(q_refs[r][:, cs], in_head)
                z = lax.dot_general(qs, k_ref[pl.ds(start, SB_KB), cs], _NT, preferred_element_type=F32)
                sp = _softplus(z)
                rt = jnp.dot(jnp.where(valid, sp, 0.0).astype(BF16), m_ref[...],
                             preferred_element_type=F32)
                c = c_scrs[r][hp]
                a = jnp.where(valid, jnp.exp(z - sp - rt[:, :SB_KB] - c), 0.0)
                o_scrs[r][hp] += jnp.dot(a.astype(BF16), v_ref[pl.ds(start, SB_KB), cs],
                                         preferred_element_type=F32)
                c = c + rt[:, SB_KB:]
                c_scrs[r][hp] = c
                cmin = c if cmin is None else jnp.minimum(cmin, c)
        return d + 1, jnp.min(cmin)

    lax.while_loop(more, one_block, (jnp.int32(0), jnp.min(cmin)))


def _sb_finish(o_scr, in_head):
    nh = len(in_head)
    parts = []
    for hp in range(o_scr.shape[0]):
        o2 = o_scr[hp]
        res = o2[(nh - 1) * SB_Q:]
        for h in range(nh - 1):
            res = jnp.where(in_head[h], o2[h * SB_Q:(h + 1) * SB_Q], res)
        parts.append(res)
    return jnp.concatenate(parts, axis=1)


def _tail_stages(x_ref, mix_in, p_ref, apn_ref, fpre_ref, fpost_ref, wout_ref, wg_ref, wu_ref,
                 wd_ref, pp_ref, pg_ref, act_scr, o_ref):
    d = wout_ref.shape[1]
    sw = d // TAIL_SLABS
    slabs = [slice(j * sw, (j + 1) * sw) for j in range(TAIL_SLABS)]
    parts = []
    for sl in slabs:
        parts.append(jnp.dot(mix_in, wout_ref[:, sl], preferred_element_type=F32))
        yield
    h1 = x_ref[...] + _rms(jnp.concatenate(parts, axis=1)) * apn_ref[...]
    u = (_rms(h1) * fpre_ref[...]).astype(BF16)
    for j in range(act_scr.shape[1] // FF_SLAB):
        fs = slice(j * FF_SLAB, (j + 1) * FF_SLAB)
        g = jnp.dot(u, wg_ref[:, fs], preferred_element_type=F32)
        up = jnp.dot(u, wu_ref[:, fs], preferred_element_type=F32)
        act_scr[:, fs] = (g * jax.nn.sigmoid(g) * up).astype(BF16)
        yield
    act = act_scr[...]
    parts = []
    for sl in slabs:
        parts.append(jnp.dot(act, wd_ref[:, sl], preferred_element_type=F32))
        yield
    h2 = h1 + _rms(jnp.concatenate(parts, axis=1)) * fpost_ref[...]
    emb = jnp.dot(p_ref[...].astype(BF16), pp_ref[...], preferred_element_type=F32)
    h2b = h2.astype(BF16)
    for sl in slabs:
        gate = jnp.dot(h2b, pg_ref[:, sl], preferred_element_type=F32)
        o_ref[:, sl] = h2[:, sl] + emb[:, sl] * jax.nn.sigmoid(gate)
        yield


def _round_robin(gens):
    live = list(gens)
    while live:
        for gen in list(live):
            try:
                next(gen)
                yield
            except StopIteration:
                live.remove(gen)


def _interleave(streams):
    state = [[gen, 0, n] for gen, n in streams]
    while state:
        cur = min(state, key=lambda t: t[1] / t[2])
        try:
            next(cur[0])
            cur[1] += 1
        except StopIteration:
            state.remove(cur)


def _fused_body(hq_ref, hf_ref, hi_ref, hg_ref, sq_ref, k_ref, v_ref, x_ref, p_ref,
                gam_ref, hgn_ref, tri_ref, esel_ref, u_ref, m_ref, sbn_ref,
                apn_ref, fpre_ref, fpost_ref, wout_ref, wg_ref, wu_ref, wd_ref, pp_ref, pg_ref,
                o_ref,
                mix_scr, st_ref, b_scr, lk_scr, k_scr, d_scr, so_scr, sc_scr, act_scr,
                *, layer, dh, blocks_per_seq):
    c = HG_CHUNK
    w = hq_ref.shape[1]
    nsub = hq_ref.shape[0] // c
    g = pl.program_id(0)
    last = pl.num_programs(0) - 1
    i = jnp.minimum(g, last - 1) % blocks_per_seq

    @pl.when(g % blocks_per_seq == 0)
    def _():
        st_ref[...] = jnp.zeros_like(st_ref)

    @pl.when(g == 0)
    def _():
        mix_scr[...] = jnp.zeros_like(mix_scr)

    slot = g % 2

    gam = gam_ref[...]
    e = jnp.exp(gam - jnp.max(gam, axis=0, keepdims=True))
    lb = jnp.sum(e[:layer + 1], axis=0, keepdims=True) / jnp.sum(e, axis=0, keepdims=True)

    def hgrn_emit(r):
        def emit(o):
            rows = pl.ds(r * c, c)
            gt = hg_ref[rows, :].astype(F32)
            mix_scr[slot, rows, 0:w] = (
                _rms(o) * hgn_ref[...] * (gt * jax.nn.sigmoid(gt))).astype(BF16)
        return emit

    hgrn = _round_robin([
        _hgrn_stages(hq_ref.at[pl.ds(r * c, c)], hf_ref.at[pl.ds(r * c, c)],
                     hi_ref.at[pl.ds(r * c, c)], lb, tri_ref, esel_ref, st_ref, b_scr.at[r],
                     lk_scr.at[r], k_scr.at[r], d_scr.at[r], hgrn_emit(r))
        for r in range(nsub)])

    in_head, col_minus_row = _sb_consts(dh)
    nq = hq_ref.shape[0] // SB_Q
    q_refs = [sq_ref.at[pl.ds(r * SB_Q, SB_Q)] for r in range(nq)]
    q0s = [(i * nq + r) * SB_Q for r in range(nq)]
    o_scrs = [so_scr.at[r] for r in range(nq)]
    c_scrs = [sc_scr.at[r] for r in range(nq)]
    cmins = []
    sb = _sb_stages(q_refs, q0s, k_ref, v_ref, u_ref, in_head, col_minus_row, o_scrs, c_scrs,
                    cmins.append)

    tail = _tail_stages(x_ref, mix_scr[1 - slot], p_ref, apn_ref, fpre_ref, fpost_ref, wout_ref,
                        wg_ref, wu_ref, wd_ref, pp_ref, pg_ref, act_scr, o_ref)
    n_tail = 3 * TAIL_SLABS + act_scr.shape[1] // FF_SLAB
    n_sb = 3 * (nq * (w // LANES) // 4) + nq * (w // LANES) // 2
    _interleave([(tail, n_tail), (hgrn, nsub * (2 + 3 * HG_HEADS)), (sb, n_sb)])

    _sb_rest(q_refs, q0s, k_ref, v_ref, m_ref, in_head, o_scrs, c_scrs, cmins[0])
    for r in range(nq):
        o = _sb_finish(o_scrs[r], in_head)
        mix_scr[slot, pl.ds(r * SB_Q, SB_Q), w:2 * w] = (_rms(o) * sbn_ref[...]).astype(BF16)


def _fused(hq, hf, hi, hg, sq, sk, sv, x2, p2, gamma, hgn, sbn, apn, fpre, fpost,
           wout, wg, wu, wd, pp, pg, *, layer, dh, tm):
    bsz, t, w = hq.shape
    n, d = x2.shape
    c = HG_CHUNK
    nsub = tm // c
    bps = t // tm
    nblk = n // tm
    tri = jnp.tril(jnp.ones((c, c), F32)).astype(BF16)
    sel = (jnp.arange(HG_SUB * LANES)[:, None] // LANES) == (jnp.arange(LANES)[None, :] % HG_SUB)
    esel = sel.astype(BF16)
    def later_and_ones(nk):
        later = jnp.arange(nk)[:, None] > jnp.arange(nk)[None, :]
        return jnp.concatenate([later, jnp.ones((nk, LANES), bool)], axis=1).astype(BF16)

    ucat, mcat = later_and_ones(SB_WIN)[:, :SB_WIN], later_and_ones(SB_KB)
    nh = LANES // dh
    nq = tm // SB_Q

    def mixer_idx(g):
        gm = jnp.minimum(g, nblk - 1)
        return (gm // bps, gm % bps, 0)

    mspec = pl.BlockSpec((None, tm, w), mixer_idx)
    kvspec = pl.BlockSpec((None, t, w), lambda g: (jnp.minimum(g, nblk - 1) // bps, 0, 0),
                          pipeline_mode=pl.Buffered(1))
    row = lambda a: pl.BlockSpec((tm, a.shape[1]), lambda g: (jnp.maximum(g - 1, 0), 0))
    consts = (gamma, hgn, tri, esel, ucat, mcat, sbn, apn, fpre, fpost, wout, wg, wu, wd, pp, pg)
    return pl.pallas_call(
        functools.partial(_fused_body, layer=layer, dh=dh, blocks_per_seq=bps),
        grid=(nblk + 1,),
        in_specs=[mspec] * 5 + [kvspec, kvspec, row(x2), row(p2)]
                 + [_const_spec(a.shape) for a in consts],
        out_specs=row(x2),
        out_shape=jax.ShapeDtypeStruct((n, d), F32),
        scratch_shapes=[
            pltpu.VMEM((2, tm, 2 * w), BF16),
            pltpu.VMEM((HG_HEADS, LANES, LANES), F32),
            pltpu.VMEM((nsub, HG_HEADS, c, LANES), F32),
            pltpu.VMEM((nsub, HG_HEADS, c, LANES), F32),
            pltpu.VMEM((nsub, HG_HEADS, c, LANES), F32),
            pltpu.VMEM((nsub, HG_HEADS, c, HG_SUB * LANES), BF16),
            pltpu.VMEM((nq, w // LANES, nh * SB_Q, LANES), F32),
            pltpu.VMEM((nq, w // LANES, nh * SB_Q, LANES), F32),
            pltpu.VMEM((tm, wd.shape[0]), BF16),
        ],
        compiler_params=pltpu.CompilerParams(
            dimension_semantics=("arbitrary",), vmem_limit_bytes=VMEM_LIMIT),
        name="mix_tail",
    )(hq, hf, hi, hg, sq, sk, sv, x2, p2, *consts)


def kernel(x, p, attn_pre_norm, w_in, hg_lower_gamma, hg_out_norm, sb_out_norm, w_out, attn_post_norm, ffn_pre_norm, w_gate_up, w_down, ffn_post_norm, ple_proj, ple_gate):
    bsz, t, d = x.shape
    depth = w_in.shape[0]
    n = bsz * t
    width = hg_out_norm.shape[1]
    assert sb_out_norm.shape[1] == width and w_in.shape[2] == 7 * width
    assert width == HG_HEADS * LANES and t >= SB_WIN
    dh = width // SB_HEADS
    d_ff = w_down.shape[1]
    assert d_ff % FF_SLAB == 0 and d % (TAIL_SLABS * LANES) == 0
    tm = 1024
    tm_fused = 256
    assert n % tm == 0 and t % tm_fused == 0 and tm_fused % HG_CHUNK == 0 and tm_fused % SB_Q == 0

    h = x.reshape(n, d)
    for i in range(depth):
        casts = [(w_out[i], 1), (w_gate_up[i], 2), (w_down[i], 1), (ple_proj[i], 1), (ple_gate[i], 1)]
        hq, hf, hi, hg, sq, sk, sv, wout, wg, wu, wd, pp, pg = _inproj(
            h, attn_pre_norm[i][None], w_in[i].astype(BF16), casts,
            width=width, sb_scale=dh ** -0.5, tm=tm)
        to3 = lambda a: a.reshape(bsz, t, width)
        h = _fused(
            to3(hq), to3(hf), to3(hi), to3(hg), to3(sq), to3(sk), to3(sv), h, p[i].reshape(n, -1),
            hg_lower_gamma, hg_out_norm[i][None], sb_out_norm[i][None],
            attn_post_norm[i][None], ffn_pre_norm[i][None], ffn_post_norm[i][None],
            wout, wg, wu, wd, pp, pg, layer=i, dh=dh, tm=tm_fused)
    return h.reshape(bsz, t, d)
```

```python
import functools

import jax
import jax.numpy as jnp
from jax import lax
from jax.experimental import pallas as pl
from jax.experimental.pallas import tpu as pltpu

F32 = jnp.float32
BF16 = jnp.bfloat16
EPS = 1e-6
LOG2E = 1.4426950408889634

HG_HEADS = 4
SB_HEADS = 8
LANES = 128
BF16_ROWS = 16
SQ_GROUP = 4
PACKED = 4
HG_CHUNK = 128
HG_SUB = 8
SB_Q = 64
SB_WIN = 256
SB_KB = 128
SB_DONE = 115.0
FF_SLAB = 256
TAIL_SLABS = 4
VMEM_LIMIT = 56 * 1024 * 1024

_NT = (((1,), (1,)), ((), ()))
_TN = (((0,), (0,)), ((), ()))


def _rms(xf):
    return xf * lax.rsqrt(jnp.mean(xf * xf, axis=-1, keepdims=True) + EPS)


def _softplus(z):
    return jnp.maximum(z, 0.0) + jnp.log(1.0 + jnp.exp2(jnp.abs(z) * -LOG2E))


def _logaddexp(a, c):
    return jnp.maximum(a, c) + jnp.log(1.0 + jnp.exp(-jnp.abs(a - c)))


def _const_spec(shape):
    nd = len(shape)
    return pl.BlockSpec(shape, lambda *_: (0,) * nd, pipeline_mode=pl.Buffered(1))


def _inproj_body(x_ref, nw_ref, w_ref, *refs, width, sb_scale, n_proj, cast_splits):
    f32_refs = refs[:len(cast_splits)]
    outs = refs[len(cast_splits):]
    u = (_rms(x_ref[...]) * nw_ref[...]).astype(BF16)
    pk_ref, hf_ref, sk_ref, sv_ref = outs[:n_proj]
    dest = {0: (pk_ref, 0), 1: (hf_ref, 0), 2: (pk_ref, 1), 3: (pk_ref, 2),
            SQ_GROUP: (pk_ref, 3), 5: (sk_ref, 0), 6: (sv_ref, 0)}
    for j, (o_ref, slot) in dest.items():
        r = jnp.dot(u, w_ref[:, j * width:(j + 1) * width], preferred_element_type=F32)
        if j == SQ_GROUP:
            r = r * sb_scale
        o_ref[:, slot * width:(slot + 1) * width] = r.astype(o_ref.dtype)

    dst = list(outs[n_proj:])
    for src_ref, parts in zip(f32_refs, cast_splits):
        blk = src_ref[...].astype(BF16)
        cw = blk.shape[1] // parts
        for k in range(parts):
            dst.pop(0)[...] = blk[:, k * cw:(k + 1) * cw]


def _cast_plan(rows, steps):
    for nb in range(steps, 0, -1):
        if steps % nb == 0 and rows % nb == 0 and (rows // nb) % BF16_ROWS == 0:
            return rows // nb, steps // nb
    raise ValueError((rows, steps))


def _inproj(x2, nw, w_in, casts, *, width, sb_scale, tm):
    n, d = x2.shape
    steps = n // tm
    out_dtypes = ((BF16, PACKED * width), (F32, width), (BF16, width), (BF16, width))
    in_specs = [pl.BlockSpec((tm, d), lambda i: (i, 0)), _const_spec((1, d)), _const_spec(w_in.shape)]
    out_specs = [pl.BlockSpec((tm, cols), lambda i: (i, 0)) for _, cols in out_dtypes]
    out_shape = [jax.ShapeDtypeStruct((n, cols), dt) for dt, cols in out_dtypes]
    for wgt, parts in casts:
        rows, cols = wgt.shape
        rb, reps = _cast_plan(rows, steps)
        idx = functools.partial(lambda i, reps: (i // reps, 0), reps=reps)
        in_specs.append(pl.BlockSpec((rb, cols), idx))
        out_specs += [pl.BlockSpec((rb, cols // parts), idx)] * parts
        out_shape += [jax.ShapeDtypeStruct((rows, cols // parts), BF16)] * parts
    return pl.pallas_call(
        functools.partial(_inproj_body, width=width, sb_scale=sb_scale, n_proj=len(out_dtypes),
                          cast_splits=tuple(parts for _, parts in casts)),
        grid=(steps,),
        in_specs=in_specs,
        out_specs=out_specs,
        out_shape=out_shape,
        compiler_params=pltpu.CompilerParams(
            dimension_semantics=("arbitrary",), vmem_limit_bytes=VMEM_LIMIT),
        name="inproj",
    )(x2, nw, w_in, *[wgt for wgt, _ in casts])


def _hgrn_stages(hq_ref, hf_ref, hi_ref, lb, tri_ref, esel_ref, st_ref, b_scr, lk_scr, k_scr, d_scr,
                 emit):
    c = HG_CHUNK
    sub = HG_SUB

    qz = hq_ref[...].astype(F32)
    q = qz * jax.nn.sigmoid(qz)
    z = hf_ref[...]
    log_sig = jnp.minimum(z, 0.0) - jnp.log(1.0 + jnp.exp2(jnp.abs(z) * -LOG2E))
    log_1mlb = jnp.log1p(-lb)
    logf = _logaddexp(jnp.log(lb), log_1mlb + log_sig) * LOG2E
    logk = (log_1mlb + log_sig - z) * LOG2E
    lhi = logf.astype(BF16)
    llo = (logf - lhi.astype(F32)).astype(BF16)
    tri = tri_ref[...]
    b = (jnp.dot(tri, lhi, preferred_element_type=F32)
         + jnp.dot(tri, llo, preferred_element_type=F32))
    lk = logk - b
    for h in range(HG_HEADS):
        cs = slice(h * LANES, (h + 1) * LANES)
        b_scr[h] = b[:, cs]
        lk_scr[h] = lk[:, cs]
        k_scr[h] = logk[:, cs]
    yield

    two = 2 * sub
    for h in range(HG_HEADS):
        cs = slice(h * LANES, (h + 1) * LANES)
        for r0 in range(0, c, two):
            bi = b[r0:r0 + two, cs]
            qi = q[r0:r0 + two, cs]
            for s in range(sub):
                def key_row(ref):
                    return jnp.concatenate([ref[h, pl.ds(r0 + s, sub, stride=0), :],
                                            ref[h, pl.ds(r0 + sub + s, sub, stride=0), :]], axis=0)
                d = qi * jnp.exp2(jnp.minimum(bi + key_row(lk_scr), key_row(k_scr)))
                d_scr[h, r0:r0 + two, s * LANES:(s + 1) * LANES] = d.astype(BF16)
            if (r0 // two) % 4 == 3:
                yield

    row = lax.broadcasted_iota(jnp.int32, (c, LANES), 0)
    col = lax.broadcasted_iota(jnp.int32, (c, LANES), 1)
    keep_diag = ((col // sub) == (row // sub)) & ((col % sub) <= (row % sub))
    half = c // 2
    rowq = lax.broadcasted_iota(jnp.int32, (half, LANES), 0)
    colq = lax.broadcasted_iota(jnp.int32, (half, LANES), 1)
    levels = []
    span = half
    while span >= sub:
        levels.append((span, ((colq // (2 * span)) == (rowq // span)) & ((colq % (2 * span)) < span)))
        span //= 2

    outs = []
    for h in range(HG_HEADS):
        cs = slice(h * LANES, (h + 1) * LANES)
        b_h, q_h, lk_h = b[:, cs], q[:, cs], lk[:, cs]
        v = hi_ref[:, cs]
        b_last = b_scr[h, c - 1:c, :]

        st = st_ref[h]
        o_h = lax.dot_general((q_h * jnp.exp2(b_h)).astype(BF16), st.astype(BF16), _NT,
                              preferred_element_type=F32)
        kh = jnp.exp2(lk_h + b_last).astype(BF16)
        st_ref[h] = st * jnp.exp2(b_last) + lax.dot_general(
            v, kh, _TN, preferred_element_type=F32)

        sdiag = jnp.where(keep_diag, jnp.dot(d_scr[h], esel_ref[...], preferred_element_type=F32), 0.0)
        pieces = [sdiag[i:i + 8] for i in range(0, c, 8)]

        for span, keep in levels:
            qparts, kparts = [], []
            for lo in range(0, c, 2 * span):
                mid, hi = lo + span, lo + 2 * span
                r = b_scr[h, mid - 1:mid, :]
                qparts.append(q_h[mid:hi] * jnp.exp2(b_h[mid:hi] - r))
                kparts.append(jnp.exp2(lk_h[lo:mid] + r))
                kparts.append(jnp.zeros((span, LANES), F32))
            sc = lax.dot_general(jnp.concatenate(qparts, axis=0).astype(BF16),
                                 jnp.concatenate(kparts, axis=0).astype(BF16), _NT,
                                 preferred_element_type=F32)
            sc = jnp.where(keep, sc, 0.0)
            for g, lo in enumerate(range(0, c, 2 * span)):
                for m in range(span // 8):
                    idx = (lo + span) // 8 + m
                    pieces[idx] = pieces[idx] + sc[g * span + 8 * m:g * span + 8 * m + 8]
        scores = jnp.concatenate(pieces, axis=0).astype(BF16)
        outs.append(o_h + jnp.dot(scores, v, preferred_element_type=F32))
        yield
    emit(jnp.concatenate(outs, axis=1))


def _sb_consts(dh):
    nh = LANES // dh
    lane = lax.broadcasted_iota(jnp.int32, (SB_Q, LANES), 1)
    in_head = [(lane >= h * dh) & (lane < (h + 1) * dh) for h in range(nh)]
    row = lax.broadcasted_iota(jnp.int32, (nh * SB_Q, SB_WIN), 0) & (SB_Q - 1)
    col = lax.broadcasted_iota(jnp.int32, (nh * SB_Q, SB_WIN), 1)
    return in_head, col - row


def _head_rows(q, in_head):
    return jnp.concatenate([jnp.where(m, q, jnp.zeros_like(q)) for m in in_head], axis=0)


def _sb_stages(q_refs, q0s, k_ref, v_ref, u_ref, in_head, col_minus_row, o_scrs, c_scrs, emit):
    npairs = q_refs[0].shape[1] // LANES
    chains = [(r, hp) for r in range(len(q_refs)) for hp in range(npairs)]
    s0s = [pl.multiple_of(jnp.maximum(q0 - (SB_WIN - SB_Q), 0), SB_Q) for q0 in q0s]
    valids = [col_minus_row < (q0 - s0) for q0, s0 in zip(q0s, s0s)]
    zs = []
    for n, (r, hp) in enumerate(chains):
        cs = slice(hp * LANES, (hp + 1) * LANES)
        zs.append(lax.dot_general(_head_rows(q_refs[r][:, cs], in_head),
                                  k_ref[pl.ds(s0s[r], SB_WIN), cs], _NT,
                                  preferred_element_type=F32))
        if n % 4 == 3:
            yield
    sps = []
    for n, z in enumerate(zs):
        sps.append(_softplus(z))
        if n % 4 == 3:
            yield
    rts = []
    for n, (r, hp) in enumerate(chains):
        spm = jnp.where(valids[r], sps[n], 0.0).astype(BF16)
        rts.append((jnp.dot(spm, u_ref[...], preferred_element_type=F32), spm))
        if n % 4 == 3:
            yield
    cmin = None
    for n, (r, hp) in enumerate(chains):
        cs = slice(hp * LANES, (hp + 1) * LANES)
        rt, spm = rts[n]
        a = jnp.where(valids[r], jnp.exp(zs[n] - sps[n] - rt), 0.0).astype(BF16)
        o_scrs[r][hp] = jnp.dot(a, v_ref[pl.ds(s0s[r], SB_WIN), cs], preferred_element_type=F32)
        c = jnp.broadcast_to(rt[:, :1] + spm[:, :1].astype(F32), (rt.shape[0], LANES))
        c_scrs[r][hp] = c
        cmin = c if cmin is None else jnp.minimum(cmin, c)
        if n % 2:
            yield
    emit(cmin)


def _sb_rest(q_refs, q0s, k_ref, v_ref, m_ref, in_head, o_scrs, c_scrs, cmin):
    npairs = q_refs[0].shape[1] // LANES
    nh = len(in_head)
    s0s = [jnp.maximum(q0 - (SB_WIN - SB_Q), 0) for q0 in q0s]
    col = lax.broadcasted_iota(jnp.int32, (nh * SB_Q, SB_KB), 1)

    def more(carry):
        d, cm = carry
        return (s0s[-1] - d * SB_KB > 0) & (cm < SB_DONE)

    def one_block(carry):
        d, _ = carry
        cmin = None
        for r in range(len(q_refs)):
            end = s0s[r] - d * SB_KB
            start = pl.multiple_of(jnp.maximum(end - SB_KB, 0), SB_Q)
            valid = col < (end - start)
            for hp in range(npairs):
                cs = slice(hp * LANES, (hp + 1) * LANES)
                qs = _head_rows(q_refs[r][:, cs], in_head)
                z = lax.dot_general(qs, k_ref[pl.ds(start, SB_KB), cs], _NT, preferred_element_type=F32)
                sp = _softplus(z)
                rt = jnp.dot(jnp.where(valid, sp, 0.0).astype(BF16), m_ref[...],
                             preferred_element_type=F32)
                c = c_scrs[r][hp]
                a = jnp.where(valid, jnp.exp(z - sp - rt[:, :SB_KB] - c), 0.0)
                o_scrs[r][hp] += jnp.dot(a.astype(BF16), v_ref[pl.ds(start, SB_KB), cs],
                                         preferred_element_type=F32)
                c = c + rt[:, SB_KB:]
                c_scrs[r][hp] = c
                cmin = c if cmin is None else jnp.minimum(cmin, c)
        return d + 1, jnp.min(cmin)

    lax.while_loop(more, one_block, (jnp.int32(0), jnp.min(cmin)))


def _sb_finish(o_scr, in_head):
    nh = len(in_head)
    parts = []
    for hp in range(o_scr.shape[0]):
        o2 = o_scr[hp]
        res = o2[(nh - 1) * SB_Q:]
        for h in range(nh - 1):
            res = jnp.where(in_head[h], o2[h * SB_Q:(h + 1) * SB_Q], res)
        parts.append(res)
    return jnp.concatenate(parts, axis=1)


def _tail_stages(x_ref, mix_in, p_ref, apn_ref, fpre_ref, fpost_ref, wout_ref, wg_ref, wu_ref,
                 wd_ref, pp_ref, pg_ref, act_scr, o_ref):
    d = wout_ref.shape[1]
    sw = d // TAIL_SLABS
    slabs = [slice(j * sw, (j + 1) * sw) for j in range(TAIL_SLABS)]
    parts = []
    for sl in slabs:
        parts.append(jnp.dot(mix_in, wout_ref[:, sl], preferred_element_type=F32))
        yield
    h1 = x_ref[...] + _rms(jnp.concatenate(parts, axis=1)) * apn_ref[...]
    u = (_rms(h1) * fpre_ref[...]).astype(BF16)
    for j in range(act_scr.shape[1] // FF_SLAB):
        fs = slice(j * FF_SLAB, (j + 1) * FF_SLAB)
        g = jnp.dot(u, wg_ref[:, fs], preferred_element_type=F32)
        up = jnp.dot(u, wu_ref[:, fs], preferred_element_type=F32)
        act_scr[:, fs] = (g * jax.nn.sigmoid(g) * up).astype(BF16)
        yield
    act = act_scr[...]
    parts = []
    for sl in slabs:
        parts.append(jnp.dot(act, wd_ref[:, sl], preferred_element_type=F32))
        yield
    h2 = h1 + _rms(jnp.concatenate(parts, axis=1)) * fpost_ref[...]
    emb = jnp.dot(p_ref[...].astype(BF16), pp_ref[...], preferred_element_type=F32)
    h2b = h2.astype(BF16)
    for sl in slabs:
        gate = jnp.dot(h2b, pg_ref[:, sl], preferred_element_type=F32)
        o_ref[:, sl] = h2[:, sl] + emb[:, sl] * jax.nn.sigmoid(gate)
        yield


def _round_robin(gens):
    live = list(gens)
    while live:
        for gen in list(live):
            try:
                next(gen)
                yield
            except StopIteration:
                live.remove(gen)


def _interleave(streams):
    state = [[gen, 0, n] for gen, n in streams]
    while state:
        cur = min(state, key=lambda t: t[1] / t[2])
        try:
            next(cur[0])
            cur[1] += 1
        except StopIteration:
            state.remove(cur)


def _fused_body(pk_ref, hf_ref, k_ref, v_ref, x_ref, p_ref,
                gam_ref, hgn_ref, tri_ref, esel_ref, u_ref, m_ref, sbn_ref,
                apn_ref, fpre_ref, fpost_ref, wout_ref, wg_ref, wu_ref, wd_ref, pp_ref, pg_ref,
                o_ref,
                mix_scr, st_ref, b_scr, lk_scr, k_scr, d_scr, so_scr, sc_scr, act_scr,
                *, layer, dh, blocks_per_seq):
    c = HG_CHUNK
    w = hf_ref.shape[1]
    rows_all = pl.ds(0, hf_ref.shape[0])
    hq_ref, hi_ref, hg_ref, sq_ref = (pk_ref.at[rows_all, pl.ds(j * w, w)] for j in range(PACKED))
    nsub = hq_ref.shape[0] // c
    g = pl.program_id(0)
    last = pl.num_programs(0) - 1
    i = jnp.minimum(g, last - 1) % blocks_per_seq

    @pl.when(g % blocks_per_seq == 0)
    def _():
        st_ref[...] = jnp.zeros_like(st_ref)

    @pl.when(g == 0)
    def _():
        mix_scr[...] = jnp.zeros_like(mix_scr)

    slot = g % 2

    gam = gam_ref[...]
    e = jnp.exp(gam - jnp.max(gam, axis=0, keepdims=True))
    lb = jnp.sum(e[:layer + 1], axis=0, keepdims=True) / jnp.sum(e, axis=0, keepdims=True)

    def hgrn_emit(r):
        def emit(o):
            rows = pl.ds(r * c, c)
            gt = hg_ref[rows, :].astype(F32)
            mix_scr[slot, rows, 0:w] = (
                _rms(o) * hgn_ref[...] * (gt * jax.nn.sigmoid(gt))).astype(BF16)
        return emit

    hgrn = _round_robin([
        _hgrn_stages(hq_ref.at[pl.ds(r * c, c)], hf_ref.at[pl.ds(r * c, c)],
                     hi_ref.at[pl.ds(r * c, c)], lb, tri_ref, esel_ref, st_ref, b_scr.at[r],
                     lk_scr.at[r], k_scr.at[r], d_scr.at[r], hgrn_emit(r))
        for r in range(nsub)])

    in_head, col_minus_row = _sb_consts(dh)
    nq = hq_ref.shape[0] // SB_Q
    q_refs = [sq_ref.at[pl.ds(r * SB_Q, SB_Q)] for r in range(nq)]
    q0s = [(i * nq + r) * SB_Q for r in range(nq)]
    o_scrs = [so_scr.at[r] for r in range(nq)]
    c_scrs = [sc_scr.at[r] for r in range(nq)]
    cmins = []
    sb = _sb_stages(q_refs, q0s, k_ref, v_ref, u_ref, in_head, col_minus_row, o_scrs, c_scrs,
                    cmins.append)

    tail = _tail_stages(x_ref, mix_scr[1 - slot], p_ref, apn_ref, fpre_ref, fpost_ref, wout_ref,
                        wg_ref, wu_ref, wd_ref, pp_ref, pg_ref, act_scr, o_ref)
    n_tail = 3 * TAIL_SLABS + act_scr.shape[1] // FF_SLAB
    n_sb = 3 * (nq * (w // LANES) // 4) + nq * (w // LANES) // 2
    _interleave([(tail, n_tail), (hgrn, nsub * (2 + 3 * HG_HEADS)), (sb, n_sb)])

    _sb_rest(q_refs, q0s, k_ref, v_ref, m_ref, in_head, o_scrs, c_scrs, cmins[0])
    for r in range(nq):
        o = _sb_finish(o_scrs[r], in_head)
        mix_scr[slot, pl.ds(r * SB_Q, SB_Q), w:2 * w] = (_rms(o) * sbn_ref[...]).astype(BF16)


def _fused(pk, hf, sk, sv, x2, p2, gamma, hgn, sbn, apn, fpre, fpost,
           wout, wg, wu, wd, pp, pg, *, layer, dh, tm):
    bsz, t, w = hf.shape
    n, d = x2.shape
    c = HG_CHUNK
    nsub = tm // c
    bps = t // tm
    nblk = n // tm
    tri = jnp.tril(jnp.ones((c, c), F32)).astype(BF16)
    sel = (jnp.arange(HG_SUB * LANES)[:, None] // LANES) == (jnp.arange(LANES)[None, :] % HG_SUB)
    esel = sel.astype(BF16)
    def later_and_ones(nk):
        later = jnp.arange(nk)[:, None] > jnp.arange(nk)[None, :]
        return jnp.concatenate([later, jnp.ones((nk, LANES), bool)], axis=1).astype(BF16)

    ucat, mcat = later_and_ones(SB_WIN)[:, :SB_WIN], later_and_ones(SB_KB)
    nh = LANES // dh
    nq = tm // SB_Q

    def mixer_idx(g):
        gm = jnp.minimum(g, nblk - 1)
        return (gm // bps, gm % bps, 0)

    mspec = lambda a: pl.BlockSpec((None, tm, a.shape[2]), mixer_idx)
    kvspec = pl.BlockSpec((None, t, w), lambda g: (jnp.minimum(g, nblk - 1) // bps, 0, 0),
                          pipeline_mode=pl.Buffered(1))
    row = lambda a: pl.BlockSpec((tm, a.shape[1]), lambda g: (jnp.maximum(g - 1, 0), 0))
    consts = (gamma, hgn, tri, esel, ucat, mcat, sbn, apn, fpre, fpost, wout, wg, wu, wd, pp, pg)
    return pl.pallas_call(
        functools.partial(_fused_body, layer=layer, dh=dh, blocks_per_seq=bps),
        grid=(nblk + 1,),
        in_specs=[mspec(pk), mspec(hf), kvspec, kvspec, row(x2), row(p2)]
                 + [_const_spec(a.shape) for a in consts],
        out_specs=row(x2),
        out_shape=jax.ShapeDtypeStruct((n, d), F32),
        scratch_shapes=[
            pltpu.VMEM((2, tm, 2 * w), BF16),
            pltpu.VMEM((HG_HEADS, LANES, LANES), F32),
            pltpu.VMEM((nsub, HG_HEADS, c, LANES), F32),
            pltpu.VMEM((nsub, HG_HEADS, c, LANES), F32),
            pltpu.VMEM((nsub, HG_HEADS, c, LANES), F32),
            pltpu.VMEM((nsub, HG_HEADS, c, HG_SUB * LANES), BF16),
            pltpu.VMEM((nq, w // LANES, nh * SB_Q, LANES), F32),
            pltpu.VMEM((nq, w // LANES, nh * SB_Q, LANES), F32),
            pltpu.VMEM((tm, wd.shape[0]), BF16),
        ],
        compiler_params=pltpu.CompilerParams(
            dimension_semantics=("arbitrary",), vmem_limit_bytes=VMEM_LIMIT),
        name="mix_tail",
    )(pk, hf, sk, sv, x2, p2, *consts)


def kernel(x, p, attn_pre_norm, w_in, hg_lower_gamma, hg_out_norm, sb_out_norm, w_out, attn_post_norm, ffn_pre_norm, w_gate_up, w_down, ffn_post_norm, ple_proj, ple_gate):
    bsz, t, d = x.shape
    depth = w_in.shape[0]
    n = bsz * t
    width = hg_out_norm.shape[1]
    assert sb_out_norm.shape[1] == width and w_in.shape[2] == 7 * width
    assert width == HG_HEADS * LANES and t >= SB_WIN
    dh = width // SB_HEADS
    d_ff = w_down.shape[1]
    assert d_ff % FF_SLAB == 0 and d % (TAIL_SLABS * LANES) == 0
    tm = 1024
    tm_fused = 256
    assert n % tm == 0 and t % tm_fused == 0 and tm_fused % HG_CHUNK == 0 and tm_fused % SB_Q == 0

    h = x.reshape(n, d)
    for i in range(depth):
        casts = [(w_out[i], 1), (w_gate_up[i], 2), (w_down[i], 1), (ple_proj[i], 1), (ple_gate[i], 1)]
        pk, hf, sk, sv, wout, wg, wu, wd, pp, pg = _inproj(
            h, attn_pre_norm[i][None], w_in[i].astype(BF16), casts,
            width=width, sb_scale=dh ** -0.5, tm=tm)
        to3 = lambda a: a.reshape(bsz, t, -1)
        h = _fused(
            to3(pk), to3(hf), to3(sk), to3(sv), h, p[i].reshape(n, -1),
            hg_lower_gamma, hg_out_norm[i][None], sb_out_norm[i][None],
            attn_post_norm[i][None], ffn_pre_norm[i][None], ffn_post_norm[i][None],
            wout, wg, wu, wd, pp, pg, layer=i, dh=dh, tm=tm_fused)
    return h.reshape(bsz, t, d)
```
